```python
import math
import jax
import jax.numpy as jnp
from jax import lax
import numpy as np


D_MODEL = 1024
BATCH = 8
SEQ = 2048
DEPTH = 1

F32 = jnp.float32
EPS = 1e-6
PLE_DIM = 256

A_HEADS = 8
A_HEAD_DIM = 64
A_WIDTH = A_HEADS * A_HEAD_DIM
MOBA_BLOCK = 256
MOBA_TOPK = 3
MOBA_QCHUNK = 16

SSD_D_INNER = D_MODEL
SSD_HEAD_DIM = 64
SSD_HEADS = SSD_D_INNER // SSD_HEAD_DIM
SSD_GROUPS = 2
SSD_D_STATE = 128
SSD_CONV = 4
SSD_CHUNK = 256
SSD_CONV_DIM = SSD_D_INNER + 2 * SSD_GROUPS * SSD_D_STATE

MOE_GROUPS = 4
MOE_EXPERTS_PER_GROUP = 4
MOE_EXPERTS = MOE_GROUPS * MOE_EXPERTS_PER_GROUP
MOE_TOPK = 2
MOE_D_FF = 512

OFF_Q = 0
OFF_K = OFF_Q + A_WIDTH
OFF_V = OFF_K + A_WIDTH
OFF_Z = OFF_V + A_WIDTH
OFF_XBC = OFF_Z + SSD_D_INNER
OFF_DT = OFF_XBC + SSD_CONV_DIM
OFF_GA = OFF_DT + SSD_HEADS
OFF_GB = OFF_GA + D_MODEL
IN_DIM = OFF_GB + D_MODEL

kernel_name = 'hybrid_moba_ssd_hmoe_block'


def rms_norm(x, g):
    xf = x.astype(F32)
    y = xf * lax.rsqrt(jnp.mean(xf * xf, axis=-1, keepdims=True) + EPS)
    return (y * g.astype(F32)).astype(x.dtype)


def alibi_slopes(n):
    return 2.0 ** (-8.0 * jnp.arange(1, n + 1, dtype=F32) / n)


def moba_attention(q, k, v):
    bsz, s, nh, hd = q.shape
    nb = -(-s // MOBA_BLOCK)
    s_pad = nb * MOBA_BLOCK
    pad = ((0, 0), (0, s_pad - s), (0, 0), (0, 0))
    q, k, v = [jnp.pad(t, pad).transpose(0, 2, 1, 3) for t in (q, k, v)]
    scale = hd ** -0.5
    kb = k.reshape(bsz, nh, nb, MOBA_BLOCK, hd)
    vb = v.reshape(bsz, nh, nb, MOBA_BLOCK, hd)
    k_mean = jnp.mean(kb.astype(F32), axis=3)
    gate = jnp.einsum('bhtd,bhnd->bhtn', q.astype(F32), k_mean)
    q_blk = jnp.arange(s_pad) // MOBA_BLOCK
    past = jnp.arange(nb)[None, :] < q_blk[:, None]
    gate = jnp.where(past[None, None], gate, -jnp.inf)
    topk = min(MOBA_TOPK, nb)
    _, top_idx = lax.top_k(gate, topk)
    sel_valid = jnp.arange(topk)[None, :] < q_blk[:, None]
    slopes = alibi_slopes(nh)[None, :, None, None]
    bi = jnp.arange(bsz)[:, None, None, None]
    hi = jnp.arange(nh)[None, :, None, None]
    offs = jnp.arange(MOBA_BLOCK)
    n_chunks = s_pad // MOBA_QCHUNK

    def chunk(c):
        t0 = c * MOBA_QCHUNK
        qc = lax.dynamic_slice_in_dim(q, t0, MOBA_QCHUNK, axis=2)
        idx = lax.dynamic_slice_in_dim(top_idx, t0, MOBA_QCHUNK, axis=2)
        valid = lax.dynamic_slice_in_dim(sel_valid, t0, MOBA_QCHUNK, axis=0)
        t = t0 + jnp.arange(MOBA_QCHUNK)
        kg = kb[bi, hi, idx]
        vg = vb[bi, hi, idx]
        s_sel = jnp.einsum('bhqd,bhqkld->bhqkl', qc, kg, preferred_element_type=F32) * scale
        kpos = idx[..., None] * MOBA_BLOCK + offs
        s_sel = s_sel - slopes[..., None] * (t[:, None, None] - kpos).astype(F32)
        s_sel = jnp.where(valid[:, :, None], s_sel, -jnp.inf)
        blk0 = (t0 // MOBA_BLOCK) * MOBA_BLOCK
        ko = lax.dynamic_slice_in_dim(k, blk0, MOBA_BLOCK, axis=2)
        vo = lax.dynamic_slice_in_dim(v, blk0, MOBA_BLOCK, axis=2)
        s_own = jnp.einsum('bhqd,bhld->bhql', qc, ko, preferred_element_type=F32) * scale
        dist = t[:, None] - (blk0 + offs)[None, :]
        s_own = jnp.where(dist >= 0, s_own - slopes * dist.astype(F32), -jnp.inf)
        sc = jnp.concatenate([s_sel.reshape(bsz, nh, MOBA_QCHUNK, topk * MOBA_BLOCK), s_own], axis=-1)
        pr = jax.nn.softmax(sc, axis=-1).astype(v.dtype)
        p_sel = pr[..., :topk * MOBA_BLOCK].reshape(bsz, nh, MOBA_QCHUNK, topk, MOBA_BLOCK)
        p_own = pr[..., topk * MOBA_BLOCK:]
        return (jnp.einsum('bhqkl,bhqkld->bhqd', p_sel, vg)
                + jnp.einsum('bhql,bhld->bhqd', p_own, vo))

    out = lax.map(chunk, jnp.arange(n_chunks))
    out = out.transpose(1, 2, 0, 3, 4).reshape(bsz, nh, s_pad, hd)[:, :, :s]
    return out.transpose(0, 2, 1, 3)


def causal_depthwise_conv(u, w, b):
    c = u.shape[-1]
    out = lax.conv_general_dilated(u, w[:, None, :].astype(u.dtype), window_strides=(1,),
                                   padding=[(SSD_CONV - 1, 0)],
                                   dimension_numbers=('NWC', 'WIO', 'NWC'),
                                   feature_group_count=c)
    return out + b


def segsum(a):
    tlen = a.shape[-1]
    cs = jnp.cumsum(a, axis=-1)
    diff = cs[..., :, None] - cs[..., None, :]
    mask = jnp.tril(jnp.ones((tlen, tlen), dtype=bool))
    return jnp.where(mask, diff, -jnp.inf)


def ssd_chunked_scan(x, dt, a, b_in, c_in):
    bsz, s, nh, hp = x.shape
    ng, ns = b_in.shape[2], b_in.shape[3]
    hpg = nh // ng
    nc = -(-s // SSD_CHUNK)
    sp = nc * SSD_CHUNK

    def padt(t):
        return jnp.pad(t.astype(F32), [(0, 0), (0, sp - s)] + [(0, 0)] * (t.ndim - 2))

    x, dt, b_in, c_in = padt(x), padt(dt), padt(b_in), padt(c_in)
    xdt = (x * dt[..., None]).reshape(bsz, nc, SSD_CHUNK, ng, hpg, hp)
    da = (dt * a.astype(F32)).reshape(bsz, nc, SSD_CHUNK, ng, hpg).transpose(0, 3, 4, 1, 2)
    bc = b_in.reshape(bsz, nc, SSD_CHUNK, ng, ns)
    cc = c_in.reshape(bsz, nc, SSD_CHUNK, ng, ns)
    da_cs = jnp.cumsum(da, axis=-1)
    decay_in = jnp.exp(segsum(da))
    cb = jnp.einsum('bclgn,bcsgn->bgcls', cc, bc)
    y_diag = jnp.einsum('bgcls,bgjcls,bcsgjp->bclgjp', cb, decay_in, xdt)
    decay_to_end = jnp.exp(da_cs[..., -1:] - da_cs)
    chunk_states = jnp.einsum('bclgn,bgjcl,bclgjp->cbgjpn', bc, decay_to_end, xdt)
    chunk_decay = jnp.exp(da_cs[..., -1]).transpose(3, 0, 1, 2)

    def carry_state(h, inp):
        st, dec = inp
        return h * dec[..., None, None] + st, h

    h0 = jnp.zeros(chunk_states.shape[1:], F32)
    _, h_in = lax.scan(carry_state, h0, (chunk_states, chunk_decay))
    y_off = jnp.einsum('bclgn,cbgjpn,bgjcl->bclgjp', cc, h_in, jnp.exp(da_cs))
    return (y_diag + y_off).reshape(bsz, sp, nh, hp)[:, :s]


def ssd_mixer(z, xbc, dt_raw, conv_w, conv_b, dt_bias, a_log, d_skip, norm_g):
    bsz, s, _ = z.shape
    xbc = jax.nn.silu(causal_depthwise_conv(xbc, conv_w, conv_b))
    xs = xbc[..., :SSD_D_INNER].reshape(bsz, s, SSD_HEADS, SSD_HEAD_DIM)
    gs = SSD_GROUPS * SSD_D_STATE
    b_in = xbc[..., SSD_D_INNER:SSD_D_INNER + gs].reshape(bsz, s, SSD_GROUPS, SSD_D_STATE)
    c_in = xbc[..., SSD_D_INNER + gs:].reshape(bsz, s, SSD_GROUPS, SSD_D_STATE)
    dt = jax.nn.softplus(dt_raw.astype(F32) + dt_bias.astype(F32))
    a = -jnp.exp(a_log.astype(F32))
    y = ssd_chunked_scan(xs, dt, a, b_in, c_in)
    y = y + d_skip.astype(F32)[:, None] * xs.astype(F32)
    u = (y.reshape(bsz, s, SSD_D_INNER) * jax.nn.silu(z.astype(F32)))
    u = u.reshape(bsz, s, SSD_GROUPS, SSD_D_INNER // SSD_GROUPS)
    u = u * lax.rsqrt(jnp.mean(u * u, axis=-1, keepdims=True) + EPS)
    return (u.reshape(bsz, s, SSD_D_INNER) * norm_g.astype(F32)).astype(z.dtype)


def hier_moe(h, w_rg, b_rg, w_re, b_re, w_gate, w_up, w_down):
    bsz, s, _ = h.shape
    hf = h.astype(F32)
    g_logits = hf @ w_rg.astype(F32) + b_rg.astype(F32)
    g_prob = jax.nn.softmax(g_logits, axis=-1)
    g_pw, g_sel = lax.top_k(g_prob, 1)
    e_logits = (hf @ w_re.astype(F32) + b_re.astype(F32)).reshape(bsz, s, MOE_GROUPS, MOE_EXPERTS_PER_GROUP)
    e_in_group = jnp.take_along_axis(e_logits, g_sel[..., None], axis=2)[:, :, 0]
    e_val, e_idx = lax.top_k(e_in_group, MOE_TOPK)
    e_w = jax.nn.softmax(e_val, axis=-1) * g_pw
    expert_id = g_sel * MOE_EXPERTS_PER_GROUP + e_idx
    combine = jnp.sum(jax.nn.one_hot(expert_id, MOE_EXPERTS, dtype=F32) * e_w[..., None], axis=-2)
    out = jnp.zeros_like(h)
    for e in range(MOE_EXPERTS):
        he = jax.nn.silu(h @ w_gate[e]) * (h @ w_up[e])
        out = out + combine[..., e:e + 1].astype(h.dtype) * (he @ w_down[e])
    return out


def setup_inputs(seed: int = 0) -> dict:
    key = jax.random.key(seed)
    ks = jax.random.split(key, 24)
    nrm = jax.random.normal

    def gain(k, n):
        return 1.0 + 0.02 * nrm(k, (DEPTH, n), F32)

    dt0 = jnp.exp(jax.random.uniform(ks[6], (DEPTH, SSD_HEADS), F32)
                  * (math.log(0.1) - math.log(1e-3)) + math.log(1e-3))
    return {
        'x': nrm(ks[0], (BATCH, SEQ, D_MODEL), F32),
        'p': nrm(ks[1], (DEPTH, BATCH, SEQ, PLE_DIM), F32),
        'mix_norm_g': gain(ks[2], D_MODEL),
        'w_in': nrm(ks[3], (DEPTH, D_MODEL, IN_DIM), F32) * D_MODEL ** -0.5,
        'conv_w': nrm(ks[4], (DEPTH, SSD_CONV, SSD_CONV_DIM), F32) * SSD_CONV ** -0.5,
        'conv_b': 0.02 * nrm(ks[5], (DEPTH, SSD_CONV_DIM), F32),
        'dt_bias': dt0 + jnp.log(-jnp.expm1(-dt0)),
        'a_log': jnp.log(jax.random.uniform(ks[7], (DEPTH, SSD_HEADS), F32, minval=1.0, maxval=16.0)),
        'd_skip': gain(ks[8], SSD_HEADS),
        'ssd_norm_g': gain(ks[9], SSD_D_INNER),
        'w_out_a': nrm(ks[10], (DEPTH, A_WIDTH, D_MODEL), F32) * A_WIDTH ** -0.5,
        'w_out_b': nrm(ks[11], (DEPTH, SSD_D_INNER, D_MODEL), F32) * SSD_D_INNER ** -0.5,
        'w_out': nrm(ks[12], (DEPTH, D_MODEL, D_MODEL), F32) * D_MODEL ** -0.5,
        'ffn_norm_g': gain(ks[13], D_MODEL),
        'w_rg': nrm(ks[14], (DEPTH, D_MODEL, MOE_GROUPS), F32) * D_MODEL ** -0.5,
        'b_rg': 0.01 * nrm(ks[15], (DEPTH, MOE_GROUPS), F32),
        'w_re': nrm(ks[16], (DEPTH, D_MODEL, MOE_EXPERTS), F32) * D_MODEL ** -0.5,
        'b_re': 0.01 * nrm(ks[17], (DEPTH, MOE_EXPERTS), F32),
        'w_gate': nrm(ks[18], (DEPTH, MOE_EXPERTS, D_MODEL, MOE_D_FF), F32) * D_MODEL ** -0.5,
        'w_up': nrm(ks[19], (DEPTH, MOE_EXPERTS, D_MODEL, MOE_D_FF), F32) * D_MODEL ** -0.5,
        'w_down': nrm(ks[20], (DEPTH, MOE_EXPERTS, MOE_D_FF, D_MODEL), F32) * MOE_D_FF ** -0.5,
        'w_ple_proj': nrm(ks[21], (DEPTH, PLE_DIM, D_MODEL), F32) * PLE_DIM ** -0.5,
        'w_ple_gate': nrm(ks[22], (DEPTH, D_MODEL, D_MODEL), F32) * D_MODEL ** -0.5,
        'final_norm_g': 1.0 + 0.02 * nrm(ks[23], (D_MODEL,), F32),
    }


def reference(x, p, mix_norm_g, w_in, conv_w, conv_b, dt_bias, a_log, d_skip, ssd_norm_g,
              w_out_a, w_out_b, w_out, ffn_norm_g, w_rg, b_rg, w_re, b_re, w_gate, w_up,
              w_down, w_ple_proj, w_ple_gate, final_norm_g):
    bsz, s, _ = x.shape
    for i in range(DEPTH):
        h = rms_norm(x, mix_norm_g[i])
        proj = h @ w_in[i]
        q = proj[..., OFF_Q:OFF_K].reshape(bsz, s, A_HEADS, A_HEAD_DIM)
        k = proj[..., OFF_K:OFF_V].reshape(bsz, s, A_HEADS, A_HEAD_DIM)
        v = proj[..., OFF_V:OFF_Z].reshape(bsz, s, A_HEADS, A_HEAD_DIM)
        z = proj[..., OFF_Z:OFF_XBC]
        xbc = proj[..., OFF_XBC:OFF_DT]
        dt_raw = proj[..., OFF_DT:OFF_GA]
        gate_a = jax.nn.sigmoid(proj[..., OFF_GA:OFF_GB])
        gate_b = jax.nn.sigmoid(proj[..., OFF_GB:IN_DIM])
        y_a = moba_attention(q, k, v).reshape(bsz, s, A_WIDTH) @ w_out_a[i]
        y_b = ssd_mixer(z, xbc, dt_raw, conv_w[i], conv_b[i], dt_bias[i], a_log[i],
                        d_skip[i], ssd_norm_g[i]) @ w_out_b[i]
        x = x + (gate_a * y_a + gate_b * y_b) @ w_out[i]
        h2 = rms_norm(x, ffn_norm_g[i])
        x = x + hier_moe(h2, w_rg[i], b_rg[i], w_re[i], b_re[i], w_gate[i], w_up[i], w_down[i])
        x = x + jax.nn.sigmoid(x @ w_ple_gate[i]) * (p[i] @ w_ple_proj[i])
    return rms_norm(x, final_norm_g)
```

```python
import functools

import jax
import jax.numpy as jnp
from jax import lax
from jax.experimental import pallas as pl
from jax.experimental.pallas import tpu as pltpu

F32 = jnp.float32
BF16 = jnp.bfloat16
EPS = 1e-6

D_MODEL = 1024
PLE_DIM = 256

A_HEADS = 8
A_HEAD_DIM = 64
A_WIDTH = A_HEADS * A_HEAD_DIM
MOBA_BLOCK = 256
MOBA_TOPK = 3

SSD_D_INNER = 1024
SSD_HEAD_DIM = 64
SSD_HEADS = 16
SSD_GROUPS = 2
SSD_D_STATE = 128
SSD_CONV = 4
SSD_CHUNK = 256
SSD_CONV_DIM = SSD_D_INNER + 2 * SSD_GROUPS * SSD_D_STATE
SSD_GROUP_WIDTH = SSD_D_INNER // SSD_GROUPS
SSD_HEADS_PER_GROUP = SSD_HEADS // SSD_GROUPS

MOE_GROUPS = 4
MOE_EXPERTS_PER_GROUP = 4
MOE_EXPERTS = 16
MOE_D_FF = 512

OFF_Q = 0
OFF_Z = 3 * A_WIDTH
OFF_XBC = OFF_Z + SSD_D_INNER
OFF_DT = OFF_XBC + SSD_CONV_DIM
OFF_GA = OFF_DT + SSD_HEADS
IN_DIM = OFF_GA + 2 * D_MODEL

LANES = 128
SUBLANES = 8
ROUTER_SLOT = 8
VMEM_LIMIT = 56 * 1024 * 1024

NEG_INF = float("-inf")


def _dot(a, b):
    return jnp.dot(a, b, preferred_element_type=F32)


def _dot_nt(a, b):
    return lax.dot_general(a, b, (((1,), (1,)), ((), ())), preferred_element_type=F32)


def _split3(x):
    hi = x.astype(BF16)
    r1 = x - hi.astype(F32)
    mid = r1.astype(BF16)
    lo = (r1 - mid.astype(F32)).astype(BF16)
    return hi, mid, lo


def _sigmoid(x):
    return 1.0 / (1.0 + jnp.exp(-x))


def _silu(x):
    return x * _sigmoid(x)


def _rms(x, g):
    ms = jnp.mean(x * x, axis=-1, keepdims=True)
    return x * lax.rsqrt(ms + EPS) * g


def _const_spec(shape):
    nd = len(shape)
    return pl.BlockSpec(shape, lambda *_: (0,) * nd)


def _params(sem):
    return pltpu.CompilerParams(dimension_semantics=sem, vmem_limit_bytes=VMEM_LIMIT)


def _inproj_kernel(x_ref, g_ref, wqkv_ref, wz_ref, wxbc_ref, wdt_ref, wg_ref,
                   qkv_ref, z_ref, xbc_ref, dt_ref, gates_ref):
    h = _rms(x_ref[...], g_ref[...]).astype(BF16)
    qkv_ref[...] = _dot(h, wqkv_ref[...]).astype(BF16)
    z_ref[...] = _dot(h, wz_ref[...]).astype(BF16)
    xbc_ref[...] = _dot(h, wxbc_ref[...]).astype(BF16)
    dt_ref[...] = _dot(h, wdt_ref[...])
    gates_ref[...] = _dot(h, wg_ref[...]).astype(BF16)


def _in_proj(x2d, g, w_in, tm=256):
    t = x2d.shape[0]
    wb = w_in.astype(BF16)
    wqkv = wb[:, OFF_Q:OFF_Z]
    wz = wb[:, OFF_Z:OFF_XBC]
    wxbc = wb[:, OFF_XBC:OFF_DT]
    wdt = jnp.pad(wb[:, OFF_DT:OFF_GA], ((0, 0), (0, LANES - SSD_HEADS)))
    wg = wb[:, OFF_GA:IN_DIM]
    widths = (3 * A_WIDTH, SSD_D_INNER, SSD_CONV_DIM, LANES, 2 * D_MODEL)
    dtypes = (BF16, BF16, BF16, F32, BF16)
    return pl.pallas_call(
        _inproj_kernel,
        grid=(t // tm,),
        in_specs=[pl.BlockSpec((tm, D_MODEL), lambda i: (i, 0)),
                  _const_spec((1, D_MODEL))]
                 + [_const_spec((D_MODEL, w)) for w in widths],
        out_specs=[pl.BlockSpec((tm, w), lambda i: (i, 0)) for w in widths],
        out_shape=[jax.ShapeDtypeStruct((t, w), d) for w, d in zip(widths, dtypes)],
        compiler_params=_params(("parallel",)),
        name="in_proj",
    )(x2d, g.reshape(1, D_MODEL), wqkv, wz, wxbc, wdt, wg)


def _moba_kernel(slopes_ref, q_ref, k_ref, v_ref, o_ref, vt_ref, km_ref, m_ref, l_ref, acc_ref):
    hp = pl.program_id(1)
    i = pl.program_id(2)
    nb = k_ref.shape[0] // MOBA_BLOCK
    blk = MOBA_BLOCK

    @pl.when(i == 0)
    def _():
        for j in range(nb):
            vj = v_ref[j * blk:(j + 1) * blk, :].astype(F32)
            vt_ref[j] = vj.T.astype(BF16)
        kf = k_ref[...].astype(F32).reshape(nb, blk, LANES)
        kmean = jnp.sum(kf, axis=1) * (1.0 / blk)
        for part, term in enumerate(_split3(kmean)):
            km_ref[part] = term

    q2 = q_ref[...]
    lane = lax.broadcasted_iota(jnp.int32, q2.shape, 1)
    kpos = lax.broadcasted_iota(jnp.int32, (blk, blk), 0)
    qpos = lax.broadcasted_iota(jnp.int32, (blk, blk), 1)
    dist = (qpos - kpos).astype(F32)
    causal = qpos >= kpos
    blk_row = lax.broadcasted_iota(jnp.int32, (nb, blk), 0)
    past = blk_row < i

    for e in range(2):
        in_head = (lane >= e * A_HEAD_DIM) & (lane < (e + 1) * A_HEAD_DIM)
        qg = jnp.where(in_head, q2, jnp.zeros_like(q2))
        qs = (qg.astype(F32) * (A_HEAD_DIM ** -0.5)).astype(BF16)
        slope = slopes_ref[2 * hp + e]
        bias = dist * (-slope)

        gates = _dot_nt(km_ref[0], qg) + _dot_nt(km_ref[1], qg) + _dot_nt(km_ref[2], qg)

        k_own = k_ref[pl.ds(pl.multiple_of(i * blk, blk), blk), :]
        s = _dot_nt(k_own, qs) + bias
        s = jnp.where(causal, s, NEG_INF)
        m0 = jnp.max(s, axis=0, keepdims=True)
        p = jnp.exp(s - m0)
        m_ref[e] = m0
        l_ref[e] = jnp.sum(p, axis=0, keepdims=True)
        vt_own = vt_ref[i]
        acc_ref[e] = _dot(vt_own[e * A_HEAD_DIM:(e + 1) * A_HEAD_DIM, :], p.astype(BF16))

        for j in range(nb - 1):
            @pl.when(j < i)
            def _(j=j, e=e, qs=qs, bias=bias, gates=gates, slope=slope):
                gj = gates[j:j + 1, :]
                beats = past & ((gates > gj) | ((gates == gj) & (blk_row < j)))
                rank = jnp.sum(beats.astype(F32), axis=0, keepdims=True)
                sel = rank < float(MOBA_TOPK)
                off = (i - j).astype(F32) * float(blk) * slope
                sj = _dot_nt(k_ref[j * blk:(j + 1) * blk, :], qs) + (bias - off)
                sj = jnp.where(sel, sj, NEG_INF)
                m_old = m_ref[e]
                m_new = jnp.maximum(m_old, jnp.max(sj, axis=0, keepdims=True))
                alpha = jnp.exp(m_old - m_new)
                pj = jnp.exp(sj - m_new)
                m_ref[e] = m_new
                l_ref[e] = l_ref[e] * alpha + jnp.sum(pj, axis=0, keepdims=True)
                vtj = vt_ref[j]
                acc_ref[e] = acc_ref[e] * alpha + _dot(
                    vtj[e * A_HEAD_DIM:(e + 1) * A_HEAD_DIM, :], pj.astype(BF16))

    out_t = jnp.concatenate([acc_ref[0] / l_ref[0], acc_ref[1] / l_ref[1]], axis=0)
    o_ref[...] = out_t.T.astype(BF16)


def _moba(qkv, bsz, seq):
    t = bsz * seq
    nb = seq // MOBA_BLOCK
    n_pairs = A_WIDTH // LANES
    slopes = 2.0 ** (-8.0 * jnp.arange(1, A_HEADS + 1, dtype=F32) / A_HEADS)
    kv_off = A_WIDTH // LANES
    return pl.pallas_call(
        _moba_kernel,
        grid=(bsz, n_pairs, nb),
        in_specs=[pl.BlockSpec(memory_space=pltpu.SMEM),
                  pl.BlockSpec((MOBA_BLOCK, LANES), lambda b, h, i: (b * nb + i, h)),
                  pl.BlockSpec((seq, LANES), lambda b, h, i: (b, kv_off + h)),
                  pl.BlockSpec((seq, LANES), lambda b, h, i: (b, 2 * kv_off + h))],
        out_specs=pl.BlockSpec((MOBA_BLOCK, LANES), lambda b, h, i: (b * nb + i, h)),
        out_shape=jax.ShapeDtypeStruct((t, A_WIDTH), BF16),
        scratch_shapes=[pltpu.VMEM((nb, LANES, MOBA_BLOCK), BF16),
                        pltpu.VMEM((3, nb, LANES), BF16),
                        pltpu.VMEM((2, 1, MOBA_BLOCK), F32),
                        pltpu.VMEM((2, 1, MOBA_BLOCK), F32),
                        pltpu.VMEM((2, A_HEAD_DIM, MOBA_BLOCK), F32)],
        compiler_params=_params(("arbitrary", "arbitrary", "arbitrary")),
        name="moba",
    )(slopes, qkv, qkv, qkv)


def _ssd_kernel(xbc_ref, dt_ref, z_ref, cw_ref, cb_ref, dtb_ref, alog_ref, dskip_ref, ng_ref,
                expand_ref, o_ref, xe_ref, state_ref):
    c = pl.program_id(1)
    L = SSD_CHUNK
    tail = SUBLANES

    @pl.when(c == 0)
    def _():
        xe_ref[0:tail, :] = jnp.zeros((tail, SSD_CONV_DIM), F32)
        state_ref[...] = jnp.zeros_like(state_ref)

    @pl.when(c > 0)
    def _():
        xe_ref[0:tail, :] = xe_ref[L:L + tail, :]

    xe_ref[tail:tail + L, :] = xbc_ref[...].astype(F32)

    first = tail - (SSD_CONV - 1)
    conv = cb_ref[...] + cw_ref[0:1, :] * xe_ref[first:first + L, :]
    for w in range(1, SSD_CONV):
        conv = conv + cw_ref[w:w + 1, :] * xe_ref[first + w:first + w + L, :]
    act = _silu(conv)
    xs = act[:, :SSD_D_INNER]
    gs = SSD_GROUPS * SSD_D_STATE

    lane = lax.broadcasted_iota(jnp.int32, (L, LANES), 1)
    head_lane = lane < SSD_HEADS
    xdt_in = dt_ref[...] + dtb_ref[...]
    dt = jnp.maximum(xdt_in, 0.0) + jnp.log1p(jnp.exp(-jnp.abs(xdt_in)))
    dt = jnp.where(head_lane, dt, 0.0)
    a = -jnp.exp(alog_ref[...])
    da = dt * a

    r = lax.broadcasted_iota(jnp.int32, (L, L), 0)
    col = lax.broadcasted_iota(jnp.int32, (L, L), 1)
    tril = r >= col
    tri = jnp.where(tril, 1.0, 0.0).astype(BF16)
    d_hi, d_mid, d_lo = _split3(da)
    cs = _dot(tri, d_hi) + _dot(tri, d_mid) + _dot(tri, d_lo)
    cs_t = cs.T
    cs_last = cs[L - 1:L, :]

    def expand(f):
        hi = f.astype(BF16)
        lo = (f - hi.astype(F32)).astype(BF16)
        return _dot(hi, expand_ref[...]) + _dot(lo, expand_ref[...])

    dt_x = expand(dt)
    to_end_x = expand(jnp.where(head_lane, jnp.exp(cs_last - cs), 0.0))
    from_start_x = expand(jnp.where(head_lane, jnp.exp(cs), 0.0))

    xdt = xs * dt_x
    xdt_b = xdt.astype(BF16)
    xdte_b = (xdt * to_end_x).astype(BF16)

    lane_c = lax.broadcasted_iota(jnp.int32, (L, LANES), 1)
    y_parts = []
    for g in range(SSD_GROUPS):
        bm = act[:, SSD_D_INNER + g * SSD_D_STATE:SSD_D_INNER + (g + 1) * SSD_D_STATE]
        cm = act[:, SSD_D_INNER + gs + g * SSD_D_STATE:SSD_D_INNER + gs + (g + 1) * SSD_D_STATE]
        bm_b = bm.astype(BF16)
        cm_b = cm.astype(BF16)
        gmat = _dot_nt(cm_b, bm_b)
        st_prev = state_ref[g]
        gw = slice(g * SSD_GROUP_WIDTH, (g + 1) * SSD_GROUP_WIDTH)
        y_off = _dot(cm_b, st_prev.astype(BF16)) * from_start_x[:, gw]
        decay = from_start_x[L - 1:L, gw]
        state_ref[g] = st_prev * decay + _dot(bm.T.astype(BF16), xdte_b[:, gw])
        for pr in range(SSD_HEADS_PER_GROUP // 2):
            colbase = g * SSD_GROUP_WIDTH + pr * LANES
            xcol = xdt_b[:, colbase:colbase + LANES]
            res = []
            for e in range(2):
                h = g * SSD_HEADS_PER_GROUP + pr * 2 + e
                diff = cs[:, h:h + 1] - cs_t[h:h + 1, :]
                lmat = jnp.exp(jnp.where(tril, diff, NEG_INF))
                res.append(_dot((gmat * lmat).astype(BF16), xcol))
            y_d = jnp.where(lane_c < SSD_HEAD_DIM, res[0], res[1])
            y_parts.append(y_d + y_off[:, pr * LANES:(pr + 1) * LANES])
    y = jnp.concatenate(y_parts, axis=1) + dskip_ref[...] * xs

    u = y * _silu(z_ref[...].astype(F32))
    outs = []
    for g in range(SSD_GROUPS):
        gw = slice(g * SSD_GROUP_WIDTH, (g + 1) * SSD_GROUP_WIDTH)
        outs.append(_rms(u[:, gw], ng_ref[:, gw]))
    o_ref[...] = jnp.concatenate(outs, axis=1).astype(BF16)


def _ssd(xbc, dt_raw, z, conv_w, conv_b, dt_bias, a_log, d_skip, norm_g, bsz, seq):
    t = bsz * seq
    nc = seq // SSD_CHUNK
    L = SSD_CHUNK
    pad = LANES - SSD_HEADS
    dtb = jnp.pad(dt_bias, (0, pad)).reshape(1, LANES)
    alog = jnp.pad(a_log, (0, pad)).reshape(1, LANES)
    dskip = jnp.repeat(d_skip, SSD_HEAD_DIM).reshape(1, SSD_D_INNER)
    head_of_chan = jnp.arange(SSD_D_INNER) // SSD_HEAD_DIM
    expand = (jnp.arange(LANES)[:, None] == head_of_chan[None, :]).astype(BF16)
    row = lambda b, c: (b * nc + c, 0)
    return pl.pallas_call(
        _ssd_kernel,
        grid=(bsz, nc),
        in_specs=[pl.BlockSpec((L, SSD_CONV_DIM), row),
                  pl.BlockSpec((L, LANES), row),
                  pl.BlockSpec((L, SSD_D_INNER), row),
                  _const_spec((SSD_CONV, SSD_CONV_DIM)),
                  _const_spec((1, SSD_CONV_DIM)),
                  _const_spec((1, LANES)),
                  _const_spec((1, LANES)),
                  _const_spec((1, SSD_D_INNER)),
                  _const_spec((1, SSD_D_INNER)),
                  _const_spec((LANES, SSD_D_INNER))],
        out_specs=pl.BlockSpec((L, SSD_D_INNER), row),
        out_shape=jax.ShapeDtypeStruct((t, SSD_D_INNER), BF16),
        scratch_shapes=[pltpu.VMEM((L + 2 * SUBLANES, SSD_CONV_DIM), F32),
                        pltpu.VMEM((SSD_GROUPS, SSD_D_STATE, SSD_GROUP_WIDTH), F32)],
        compiler_params=_params(("arbitrary", "arbitrary")),
        name="ssd",
    )(xbc, dt_raw, z, conv_w, conv_b.reshape(1, -1), dtb, alog, dskip,
      norm_g.reshape(1, -1), expand)


def _first_max(vals, rows, n_rows):
    vmax = jnp.max(vals, axis=0, keepdims=True)
    idx = jnp.min(jnp.where(vals == vmax, rows, float(n_rows)), axis=0, keepdims=True)
    return vmax, idx


def _merge_kernel(x_ref, attn_ref, ssd_ref, gates_ref, woa_ref, wob_ref, wo_ref, ng_ref,
                  wr_ref, br_ref, x1_ref, h2_ref, comb_ref):
    tm = x_ref.shape[0]
    y_a = _dot(attn_ref[...], woa_ref[...])
    y_b = _dot(ssd_ref[...], wob_ref[...])
    gates = gates_ref[...].astype(F32)
    mix = _sigmoid(gates[:, :D_MODEL]) * y_a + _sigmoid(gates[:, D_MODEL:]) * y_b
    x1 = x_ref[...] + _dot(mix.astype(BF16), wo_ref[...])
    x1_ref[...] = x1
    h2 = _rms(x1, ng_ref[...])
    h2_ref[...] = h2.astype(BF16)

    h_hi = h2.astype(BF16)
    h_lo = (h2 - h_hi.astype(F32)).astype(BF16)
    logits = (_dot(h_hi, wr_ref[0]) + _dot(h_lo, wr_ref[0]) + _dot(h_hi, wr_ref[1])
              + br_ref[...])
    lt = logits.T
    S = ROUTER_SLOT
    rows = lax.broadcasted_iota(jnp.int32, (S, tm), 0).astype(F32)
    real = rows < float(MOE_GROUPS)
    gl = jnp.where(real, lt[0:S, :], NEG_INF)
    gmax = jnp.max(gl, axis=0, keepdims=True)
    ge = jnp.exp(gl - gmax)
    gprob = ge / jnp.sum(ge, axis=0, keepdims=True)
    g_pw, g_sel = _first_max(gprob, rows, S)
    el = jnp.zeros((S, tm), F32)
    for g in range(MOE_GROUPS):
        el = jnp.where(g_sel == float(g), lt[(1 + g) * S:(2 + g) * S, :], el)
    el = jnp.where(real, el, NEG_INF)
    v1, i1 = _first_max(el, rows, S)
    el2 = jnp.where(rows == i1, NEG_INF, el)
    v2, i2 = _first_max(el2, rows, S)
    e2 = jnp.exp(v2 - v1)
    denom = 1.0 + e2
    w1 = (1.0 / denom) * g_pw
    w2 = (e2 / denom) * g_pw
    id1 = g_sel * float(MOE_EXPERTS_PER_GROUP) + i1
    id2 = g_sel * float(MOE_EXPERTS_PER_GROUP) + i2
    erow = lax.broadcasted_iota(jnp.int32, (LANES, tm), 0).astype(F32)
    comb_t = jnp.where(erow == id1, w1, 0.0) + jnp.where(erow == id2, w2, 0.0)
    comb_ref[...] = comb_t.T


def _router_weights(w_rg, b_rg, w_re, b_re):
    S = ROUTER_SLOT
    w = jnp.zeros((D_MODEL, LANES), F32).at[:, 0:MOE_GROUPS].set(w_rg)
    b = jnp.zeros((LANES,), F32).at[0:MOE_GROUPS].set(b_rg)
    for g in range(MOE_GROUPS):
        src = slice(g * MOE_EXPERTS_PER_GROUP, (g + 1) * MOE_EXPERTS_PER_GROUP)
        dst = slice((1 + g) * S, (1 + g) * S + MOE_EXPERTS_PER_GROUP)
        w = w.at[:, dst].set(w_re[:, src])
        b = b.at[dst].set(b_re[src])
    hi = w.astype(BF16)
    lo = (w - hi.astype(F32)).astype(BF16)
    return jnp.stack([hi, lo]), b.reshape(1, LANES)


def _merge(x2d, attn, ssd_out, gates, w_out_a, w_out_b, w_out, ffn_norm_g, wr, br, tm=256):
    t = x2d.shape[0]
    row = lambda i: (i, 0)
    return pl.pallas_call(
        _merge_kernel,
        grid=(t // tm,),
        in_specs=[pl.BlockSpec((tm, D_MODEL), row),
                  pl.BlockSpec((tm, A_WIDTH), row),
                  pl.BlockSpec((tm, SSD_D_INNER), row),
                  pl.BlockSpec((tm, 2 * D_MODEL), row),
                  _const_spec((A_WIDTH, D_MODEL)),
                  _const_spec((SSD_D_INNER, D_MODEL)),
                  _const_spec((D_MODEL, D_MODEL)),
                  _const_spec((1, D_MODEL)),
                  _const_spec((2, D_MODEL, LANES)),
                  _const_spec((1, LANES))],
        out_specs=[pl.BlockSpec((tm, D_MODEL), row),
                   pl.BlockSpec((tm, D_MODEL), row),
                   pl.BlockSpec((tm, LANES), row)],
        out_shape=[jax.ShapeDtypeStruct((t, D_MODEL), F32),
                   jax.ShapeDtypeStruct((t, D_MODEL), BF16),
                   jax.ShapeDtypeStruct((t, LANES), F32)],
        compiler_params=_params(("parallel",)),
        name="merge_router",
    )(x2d, attn, ssd_out, gates, w_out_a.astype(BF16), w_out_b.astype(BF16),
      w_out.astype(BF16), ffn_norm_g.reshape(1, -1), wr, br)


def _moe_kernel(h_ref, comb_ref, wg_ref, wu_ref, wd_ref, o_ref, acc_ref):
    e = pl.program_id(1)

    @pl.when(e == 0)
    def _():
        acc_ref[...] = jnp.zeros_like(acc_ref)

    h = h_ref[...]
    he = _silu(_dot(h, wg_ref[0])) * _dot(h, wu_ref[0])
    comb = comb_ref[...]
    lane = lax.broadcasted_iota(jnp.int32, comb.shape, 1)
    ce = jnp.sum(jnp.where(lane == e, comb, 0.0), axis=1, keepdims=True)
    acc_ref[...] += _dot((he * ce).astype(BF16), wd_ref[0])

    @pl.when(e == MOE_EXPERTS - 1)
    def _():
        o_ref[...] = acc_ref[...]


def _moe(h2, comb, w_gate, w_up, w_down, tm=1024):
    t = h2.shape[0]
    return pl.pallas_call(
        _moe_kernel,
        grid=(t // tm, MOE_EXPERTS),
        in_specs=[pl.BlockSpec((tm, D_MODEL), lambda i, e: (i, 0)),
                  pl.BlockSpec((tm, LANES), lambda i, e: (i, 0)),
                  pl.BlockSpec((1, D_MODEL, MOE_D_FF), lambda i, e: (e, 0, 0)),
                  pl.BlockSpec((1, D_MODEL, MOE_D_FF), lambda i, e: (e, 0, 0)),
                  pl.BlockSpec((1, MOE_D_FF, D_MODEL), lambda i, e: (e, 0, 0))],
        out_specs=pl.BlockSpec((tm, D_MODEL), lambda i, e: (i, 0)),
        out_shape=jax.ShapeDtypeStruct((t, D_MODEL), F32),
        scratch_shapes=[pltpu.VMEM((tm, D_MODEL), F32)],
        compiler_params=_params(("parallel", "arbitrary")),
        name="moe",
    )(h2, comb, w_gate.astype(BF16), w_up.astype(BF16), w_down.astype(BF16))


def _final_kernel(x1_ref, moe_ref, p_ref, wpg_ref, wpp_ref, g_ref, o_ref):
    x2 = x1_ref[...] + moe_ref[...]
    gate = _sigmoid(_dot(x2.astype(BF16), wpg_ref[...]))
    emb = _dot(p_ref[...].astype(BF16), wpp_ref[...])
    o_ref[...] = _rms(x2 + gate * emb, g_ref[...])


def _final(x1, moe_out, p2d, w_ple_gate, w_ple_proj, final_norm_g, tm=512):
    t = x1.shape[0]
    row = lambda i: (i, 0)
    return pl.pallas_call(
        _final_kernel,
        grid=(t // tm,),
        in_specs=[pl.BlockSpec((tm, D_MODEL), row),
                  pl.BlockSpec((tm, D_MODEL), row),
                  pl.BlockSpec((tm, PLE_DIM), row),
                  _const_spec((D_MODEL, D_MODEL)),
                  _const_spec((PLE_DIM, D_MODEL)),
                  _const_spec((1, D_MODEL))],
        out_specs=pl.BlockSpec((tm, D_MODEL), row),
        out_shape=jax.ShapeDtypeStruct((t, D_MODEL), F32),
        compiler_params=_params(("parallel",)),
        name="final",
    )(x1, moe_out, p2d, w_ple_gate.astype(BF16), w_ple_proj.astype(BF16),
      final_norm_g.reshape(1, -1))


def kernel(x, p, mix_norm_g, w_in, conv_w, conv_b, dt_bias, a_log, d_skip, ssd_norm_g, w_out_a, w_out_b, w_out, ffn_norm_g, w_rg, b_rg, w_re, b_re, w_gate, w_up, w_down, w_ple_proj, w_ple_gate, final_norm_g):
    bsz, seq, _ = x.shape
    assert w_in.shape[0] == 1, "single-layer block only"
    t = bsz * seq
    xr = x.reshape(t, D_MODEL)
    for i in range(1):
        qkv, z, xbc, dt_raw, gates = _in_proj(xr, mix_norm_g[i], w_in[i])
        attn = _moba(qkv, bsz, seq)
        ssd_out = _ssd(xbc, dt_raw, z, conv_w[i], conv_b[i], dt_bias[i], a_log[i],
                       d_skip[i], ssd_norm_g[i], bsz, seq)
        wr, br = _router_weights(w_rg[i], b_rg[i], w_re[i], b_re[i])
        x1, h2, comb = _merge(xr, attn, ssd_out, gates, w_out_a[i], w_out_b[i], w_out[i],
                              ffn_norm_g[i], wr, br)
        moe_out = _moe(h2, comb, w_gate[i], w_up[i], w_down[i])
        xr = _final(x1, moe_out, p[i].reshape(t, PLE_DIM), w_ple_gate[i], w_ple_proj[i],
                    final_norm_g)
    return xr.reshape(bsz, seq, D_MODEL)
```

```python
import functools

import jax
import jax.numpy as jnp
from jax import lax
from jax.experimental import pallas as pl
from jax.experimental.pallas import tpu as pltpu

F32 = jnp.float32
BF16 = jnp.bfloat16
EPS = 1e-6

D_MODEL = 1024
PLE_DIM = 256

A_HEADS = 8
A_HEAD_DIM = 64
A_WIDTH = A_HEADS * A_HEAD_DIM
MOBA_BLOCK = 256
MOBA_TOPK = 3

SSD_D_INNER = 1024
SSD_HEAD_DIM = 64
SSD_HEADS = 16
SSD_GROUPS = 2
SSD_D_STATE = 128
SSD_CONV = 4
SSD_CHUNK = 256
SSD_CONV_DIM = SSD_D_INNER + 2 * SSD_GROUPS * SSD_D_STATE
SSD_GROUP_WIDTH = SSD_D_INNER // SSD_GROUPS
SSD_HEADS_PER_GROUP = SSD_HEADS // SSD_GROUPS

MOE_GROUPS = 4
MOE_EXPERTS_PER_GROUP = 4
MOE_EXPERTS = 16
MOE_D_FF = 512

OFF_Q = 0
OFF_Z = 3 * A_WIDTH
OFF_XBC = OFF_Z + SSD_D_INNER
OFF_DT = OFF_XBC + SSD_CONV_DIM
OFF_GA = OFF_DT + SSD_HEADS
IN_DIM = OFF_GA + 2 * D_MODEL

LANES = 128
SUBLANES = 8
ROUTER_SLOT = 8
VMEM_LIMIT = 56 * 1024 * 1024

NEG_INF = float("-inf")


def _dot(a, b):
    return jnp.dot(a, b, preferred_element_type=F32)


def _dot_nt(a, b):
    return lax.dot_general(a, b, (((1,), (1,)), ((), ())), preferred_element_type=F32)


def _split3(x):
    hi = x.astype(BF16)
    r1 = x - hi.astype(F32)
    mid = r1.astype(BF16)
    lo = (r1 - mid.astype(F32)).astype(BF16)
    return hi, mid, lo


def _sigmoid(x):
    return 1.0 / (1.0 + jnp.exp(-x))


def _silu(x):
    return x * _sigmoid(x)


def _rms(x, g):
    ms = jnp.mean(x * x, axis=-1, keepdims=True)
    return x * lax.rsqrt(ms + EPS) * g


def _const_spec(shape):
    nd = len(shape)
    return pl.BlockSpec(shape, lambda *_: (0,) * nd)


def _params(sem):
    return pltpu.CompilerParams(dimension_semantics=sem, vmem_limit_bytes=VMEM_LIMIT)


def _inproj_kernel(x_ref, g_ref, wqkv_ref, wz_ref, wxbc_ref, wdt_ref, wg_ref,
                   qkv_ref, z_ref, xbc_ref, dt_ref, gates_ref):
    h = _rms(x_ref[...], g_ref[...]).astype(BF16)
    qkv_ref[...] = _dot(h, wqkv_ref[...]).astype(BF16)
    z_ref[...] = _dot(h, wz_ref[...]).astype(BF16)
    xbc_ref[...] = _dot(h, wxbc_ref[...]).astype(BF16)
    dt_ref[...] = _dot(h, wdt_ref[...])
    gates_ref[...] = _dot(h, wg_ref[...]).astype(BF16)


def _in_proj(x2d, g, w_in, tm=256):
    t = x2d.shape[0]
    wb = w_in.astype(BF16)
    wqkv = wb[:, OFF_Q:OFF_Z]
    wz = wb[:, OFF_Z:OFF_XBC]
    wxbc = wb[:, OFF_XBC:OFF_DT]
    wdt = jnp.pad(wb[:, OFF_DT:OFF_GA], ((0, 0), (0, LANES - SSD_HEADS)))
    wg = wb[:, OFF_GA:IN_DIM]
    widths = (3 * A_WIDTH, SSD_D_INNER, SSD_CONV_DIM, LANES, 2 * D_MODEL)
    dtypes = (BF16, BF16, BF16, F32, BF16)
    return pl.pallas_call(
        _inproj_kernel,
        grid=(t // tm,),
        in_specs=[pl.BlockSpec((tm, D_MODEL), lambda i: (i, 0)),
                  _const_spec((1, D_MODEL))]
                 + [_const_spec((D_MODEL, w)) for w in widths],
        out_specs=[pl.BlockSpec((tm, w), lambda i: (i, 0)) for w in widths],
        out_shape=[jax.ShapeDtypeStruct((t, w), d) for w, d in zip(widths, dtypes)],
        compiler_params=_params(("parallel",)),
        name="in_proj",
    )(x2d, g.reshape(1, D_MODEL), wqkv, wz, wxbc, wdt, wg)


MOBA_GATE_ROWS = 32
MOBA_K_LANES = 2 * LANES


def _moba_kernel(slopes_ref, q_ref, k_ref, v_ref, o_ref, vt_ref, kc_ref, s_ref):
    hp = pl.program_id(1)
    seq = k_ref.shape[0]
    nb = seq // MOBA_BLOCK
    blk = MOBA_BLOCK
    hd = A_HEAD_DIM
    g0 = MOBA_GATE_ROWS
    slopes = [slopes_ref[2 * hp + e] for e in range(2)]

    for j in range(nb):
        vt_ref[j] = v_ref[j * blk:(j + 1) * blk, :].astype(F32).T.astype(BF16)
    kf = k_ref[...].astype(F32).reshape(nb, blk, LANES)
    kmean = jnp.sum(kf, axis=1) * (1.0 / blk)
    km_terms = [t.astype(F32) for t in _split3(kmean)] + [jnp.zeros((g0 - 3 * nb, LANES), F32)]
    kc_ref[0:g0, 0:LANES] = jnp.concatenate(km_terms, axis=0).astype(BF16)
    kc_ref[0:g0, LANES:MOBA_K_LANES] = jnp.zeros((g0, LANES), BF16)
    kc_ref[g0:g0 + seq, 0:LANES] = k_ref[...]
    row = lax.broadcasted_iota(jnp.int32, (seq, LANES), 0)
    alane = lax.broadcasted_iota(jnp.int32, (seq, LANES), 1)
    k_local = jnp.bitwise_and(row, blk - 1).astype(F32)
    aug = jnp.zeros((seq, LANES), F32)
    for e in range(2):
        val = k_local * slopes[e]
        hi = val.astype(BF16).astype(F32)
        aug = jnp.where(alane == e, hi, jnp.where(alane == 2 + e, val - hi, aug))
    kc_ref[g0:g0 + seq, LANES:MOBA_K_LANES] = aug.astype(BF16)

    qlane = lax.broadcasted_iota(jnp.int32, (blk, LANES), 1)
    kpos = lax.broadcasted_iota(jnp.int32, (blk, 2 * blk), 0)
    qpos = jnp.bitwise_and(lax.broadcasted_iota(jnp.int32, (blk, 2 * blk), 1), blk - 1)
    causal = qpos >= kpos
    q_local = lax.broadcasted_iota(jnp.int32, (1, blk), 1).astype(F32)
    blk_row = lax.broadcasted_iota(jnp.int32, (nb, blk), 0)

    def scores(i):
        q2 = q_ref[i * blk:(i + 1) * blk, :]
        halves = []
        for e in range(2):
            in_head = (qlane >= e * hd) & (qlane < (e + 1) * hd)
            left = jnp.where(in_head, q2.astype(F32) * (hd ** -0.5), 0.0).astype(BF16)
            right = jnp.where((qlane == e) | (qlane == 2 + e), 1.0, 0.0).astype(BF16)
            halves.append(jnp.concatenate([left, right], axis=1))
        q_both = jnp.concatenate(halves, axis=0)
        nk = (i + 1) * blk
        s_ext = _dot_nt(kc_ref[0:g0 + nk, :], q_both)
        slot0 = i * (i + 1) // 2
        for j in range(i + 1):
            sj = s_ext[g0 + j * blk:g0 + (j + 1) * blk, :]
            if j == i:
                sj = jnp.where(causal, sj, NEG_INF)
            s_ref[slot0 + j] = sj
        past = blk_row < i
        shifts = []
        for e in range(2):
            cols = slice(e * blk, (e + 1) * blk)
            gates = s_ext[0:nb, cols] + s_ext[nb:2 * nb, cols] + s_ext[2 * nb:3 * nb, cols]
            row_bias = []
            for j in range(i + 1):
                rb = (q_local + float((i - j) * blk)) * (-slopes[e])
                if j < i:
                    gj = gates[j:j + 1, :]
                    beats = past & ((gates > gj) | ((gates == gj) & (blk_row < j)))
                    rank = jnp.sum(beats.astype(F32), axis=0, keepdims=True)
                    rb = rb + jnp.where(rank < float(MOBA_TOPK), 0.0, NEG_INF)
                row_bias.append(rb)
            m = None
            for j in range(i + 1):
                bm = jnp.max(s_ref[slot0 + j][:, cols], axis=0, keepdims=True) + row_bias[j]
                m = bm if m is None else jnp.maximum(m, bm)
            shifts.append([rb - m for rb in row_bias])
        return shifts

    def outputs(i, shifts):
        slot0 = i * (i + 1) // 2
        outs = []
        for e in range(2):
            cols = slice(e * blk, (e + 1) * blk)
            l = jnp.zeros((1, blk), F32)
            acc = jnp.zeros((hd, blk), F32)
            for j in range(i + 1):
                p = jnp.exp(s_ref[slot0 + j][:, cols] + shifts[e][j])
                l = l + jnp.sum(p, axis=0, keepdims=True)
                acc = acc + _dot(vt_ref[j][e * hd:(e + 1) * hd, :], p.astype(BF16))
            outs.append(acc / l)
        o_ref[i * blk:(i + 1) * blk, :] = jnp.concatenate(outs, axis=0).T.astype(BF16)

    shifts = scores(0)
    for i in range(nb):
        nxt = scores(i + 1) if i + 1 < nb else None
        outputs(i, shifts)
        shifts = nxt


def _moba(qkv, bsz, seq):
    t = bsz * seq
    nb = seq // MOBA_BLOCK
    n_pairs = A_WIDTH // LANES
    slopes = 2.0 ** (-8.0 * jnp.arange(1, A_HEADS + 1, dtype=F32) / A_HEADS)
    kv_off = A_WIDTH // LANES
    return pl.pallas_call(
        _moba_kernel,
        grid=(bsz, n_pairs),
        in_specs=[pl.BlockSpec(memory_space=pltpu.SMEM),
                  pl.BlockSpec((seq, LANES), lambda b, h: (b, h)),
                  pl.BlockSpec((seq, LANES), lambda b, h: (b, kv_off + h)),
                  pl.BlockSpec((seq, LANES), lambda b, h: (b, 2 * kv_off + h))],
        out_specs=pl.BlockSpec((seq, LANES), lambda b, h: (b, h)),
        out_shape=jax.ShapeDtypeStruct((t, A_WIDTH), BF16),
        scratch_shapes=[pltpu.VMEM((nb, LANES, MOBA_BLOCK), BF16),
                        pltpu.VMEM((MOBA_GATE_ROWS + seq, MOBA_K_LANES), BF16),
                        pltpu.VMEM((nb * (nb + 1) // 2, MOBA_BLOCK, 2 * MOBA_BLOCK), F32)],
        compiler_params=_params(("parallel", "parallel")),
        name="moba",
    )(slopes, qkv, qkv, qkv)


def _ssd_kernel(xbc_ref, dt_ref, z_ref, cw_ref, cb_ref, dtb_ref, alog_ref, dskip_ref, ng_ref,
                expand_ref, o_ref, xe_ref, state_ref):
    c = pl.program_id(1)
    L = SSD_CHUNK
    tail = SUBLANES

    @pl.when(c == 0)
    def _():
        xe_ref[0:tail, :] = jnp.zeros((tail, SSD_CONV_DIM), F32)
        state_ref[...] = jnp.zeros_like(state_ref)

    @pl.when(c > 0)
    def _():
        xe_ref[0:tail, :] = xe_ref[L:L + tail, :]

    xe_ref[tail:tail + L, :] = xbc_ref[...].astype(F32)

    first = tail - (SSD_CONV - 1)
    conv = cb_ref[...] + cw_ref[0:1, :] * xe_ref[first:first + L, :]
    for w in range(1, SSD_CONV):
        conv = conv + cw_ref[w:w + 1, :] * xe_ref[first + w:first + w + L, :]
    act = _silu(conv)
    xs = act[:, :SSD_D_INNER]
    gs = SSD_GROUPS * SSD_D_STATE

    lane = lax.broadcasted_iota(jnp.int32, (L, LANES), 1)
    head_lane = lane < SSD_HEADS
    xdt_in = dt_ref[...] + dtb_ref[...]
    dt = jnp.maximum(xdt_in, 0.0) + jnp.log1p(jnp.exp(-jnp.abs(xdt_in)))
    dt = jnp.where(head_lane, dt, 0.0)
    a = -jnp.exp(alog_ref[...])
    da = dt * a

    r = lax.broadcasted_iota(jnp.int32, (L, L), 0)
    col = lax.broadcasted_iota(jnp.int32, (L, L), 1)
    tril = r >= col
    tri = jnp.where(tril, 1.0, 0.0).astype(BF16)
    d_hi, d_mid, d_lo = _split3(da)
    cs = _dot(tri, d_hi) + _dot(tri, d_mid) + _dot(tri, d_lo)
    cs_t = cs.T
    cs_last = cs[L - 1:L, :]

    def expand(f):
        hi = f.astype(BF16)
        lo = (f - hi.astype(F32)).astype(BF16)
        return _dot(hi, expand_ref[...]) + _dot(lo, expand_ref[...])

    dt_x = expand(dt)
    to_end_x = expand(jnp.where(head_lane, jnp.exp(cs_last - cs), 0.0))
    from_start_x = expand(jnp.where(head_lane, jnp.exp(cs), 0.0))

    xdt = xs * dt_x
    xdt_b = xdt.astype(BF16)
    xdte_b = (xdt * to_end_x).astype(BF16)

    lane_c = lax.broadcasted_iota(jnp.int32, (L, LANES), 1)
    y_parts = []
    for g in range(SSD_GROUPS):
        bm = act[:, SSD_D_INNER + g * SSD_D_STATE:SSD_D_INNER + (g + 1) * SSD_D_STATE]
        cm = act[:, SSD_D_INNER + gs + g * SSD_D_STATE:SSD_D_INNER + gs + (g + 1) * SSD_D_STATE]
        bm_b = bm.astype(BF16)
        cm_b = cm.astype(BF16)
        gmat = _dot_nt(cm_b, bm_b)
        st_prev = state_ref[g]
        gw = slice(g * SSD_GROUP_WIDTH, (g + 1) * SSD_GROUP_WIDTH)
        y_off = _dot(cm_b, st_prev.astype(BF16)) * from_start_x[:, gw]
        decay = from_start_x[L - 1:L, gw]
        state_ref[g] = st_prev * decay + _dot(bm.T.astype(BF16), xdte_b[:, gw])
        for pr in range(SSD_HEADS_PER_GROUP // 2):
            colbase = g * SSD_GROUP_WIDTH + pr * LANES
            xcol = xdt_b[:, colbase:colbase + LANES]
            res = []
            for e in range(2):
                h = g * SSD_HEADS_PER_GROUP + pr * 2 + e
                diff = cs[:, h:h + 1] - cs_t[h:h + 1, :]
                lmat = jnp.exp(jnp.where(tril, diff, NEG_INF))
                res.append(_dot((gmat * lmat).astype(BF16), xcol))
            y_d = jnp.where(lane_c < SSD_HEAD_DIM, res[0], res[1])
            y_parts.append(y_d + y_off[:, pr * LANES:(pr + 1) * LANES])
    y = jnp.concatenate(y_parts, axis=1) + dskip_ref[...] * xs

    u = y * _silu(z_ref[...].astype(F32))
    outs = []
    for g in range(SSD_GROUPS):
        gw = slice(g * SSD_GROUP_WIDTH, (g + 1) * SSD_GROUP_WIDTH)
        outs.append(_rms(u[:, gw], ng_ref[:, gw]))
    o_ref[...] = jnp.concatenate(outs, axis=1).astype(BF16)


def _ssd(xbc, dt_raw, z, conv_w, conv_b, dt_bias, a_log, d_skip, norm_g, bsz, seq):
    t = bsz * seq
    nc = seq // SSD_CHUNK
    L = SSD_CHUNK
    pad = LANES - SSD_HEADS
    dtb = jnp.pad(dt_bias, (0, pad)).reshape(1, LANES)
    alog = jnp.pad(a_log, (0, pad)).reshape(1, LANES)
    dskip = jnp.repeat(d_skip, SSD_HEAD_DIM).reshape(1, SSD_D_INNER)
    head_of_chan = jnp.arange(SSD_D_INNER) // SSD_HEAD_DIM
    expand = (jnp.arange(LANES)[:, None] == head_of_chan[None, :]).astype(BF16)
    row = lambda b, c: (b * nc + c, 0)
    return pl.pallas_call(
        _ssd_kernel,
        grid=(bsz, nc),
        in_specs=[pl.BlockSpec((L, SSD_CONV_DIM), row),
                  pl.BlockSpec((L, LANES), row),
                  pl.BlockSpec((L, SSD_D_INNER), row),
                  _const_spec((SSD_CONV, SSD_CONV_DIM)),
                  _const_spec((1, SSD_CONV_DIM)),
                  _const_spec((1, LANES)),
                  _const_spec((1, LANES)),
                  _const_spec((1, SSD_D_INNER)),
                  _const_spec((1, SSD_D_INNER)),
                  _const_spec((LANES, SSD_D_INNER))],
        out_specs=pl.BlockSpec((L, SSD_D_INNER), row),
        out_shape=jax.ShapeDtypeStruct((t, SSD_D_INNER), BF16),
        scratch_shapes=[pltpu.VMEM((L + 2 * SUBLANES, SSD_CONV_DIM), F32),
                        pltpu.VMEM((SSD_GROUPS, SSD_D_STATE, SSD_GROUP_WIDTH), F32)],
        compiler_params=_params(("arbitrary", "arbitrary")),
        name="ssd",
    )(xbc, dt_raw, z, conv_w, conv_b.reshape(1, -1), dtb, alog, dskip,
      norm_g.reshape(1, -1), expand)


def _first_max(vals, rows, n_rows):
    vmax = jnp.max(vals, axis=0, keepdims=True)
    idx = jnp.min(jnp.where(vals == vmax, rows, float(n_rows)), axis=0, keepdims=True)
    return vmax, idx


def _merge_kernel(x_ref, attn_ref, ssd_ref, gates_ref, woa_ref, wob_ref, wo_ref, ng_ref,
                  wr_ref, br_ref, x1_ref, h2_ref, comb_ref):
    tm = x_ref.shape[0]
    y_a = _dot(attn_ref[...], woa_ref[...])
    y_b = _dot(ssd_ref[...], wob_ref[...])
    gates = gates_ref[...].astype(F32)
    mix = _sigmoid(gates[:, :D_MODEL]) * y_a + _sigmoid(gates[:, D_MODEL:]) * y_b
    x1 = x_ref[...] + _dot(mix.astype(BF16), wo_ref[...])
    x1_ref[...] = x1
    h2 = _rms(x1, ng_ref[...])
    h2_ref[...] = h2.astype(BF16)

    h_hi = h2.astype(BF16)
    h_lo = (h2 - h_hi.astype(F32)).astype(BF16)
    logits = (_dot(h_hi, wr_ref[0]) + _dot(h_lo, wr_ref[0]) + _dot(h_hi, wr_ref[1])
              + br_ref[...])
    lt = logits.T
    S = ROUTER_SLOT
    rows = lax.broadcasted_iota(jnp.int32, (S, tm), 0).astype(F32)
    real = rows < float(MOE_GROUPS)
    gl = jnp.where(real, lt[0:S, :], NEG_INF)
    gmax = jnp.max(gl, axis=0, keepdims=True)
    ge = jnp.exp(gl - gmax)
    gprob = ge / jnp.sum(ge, axis=0, keepdims=True)
    g_pw, g_sel = _first_max(gprob, rows, S)
    el = jnp.zeros((S, tm), F32)
    for g in range(MOE_GROUPS):
        el = jnp.where(g_sel == float(g), lt[(1 + g) * S:(2 + g) * S, :], el)
    el = jnp.where(real, el, NEG_INF)
    v1, i1 = _first_max(el, rows, S)
    el2 = jnp.where(rows == i1, NEG_INF, el)
    v2, i2 = _first_max(el2, rows, S)
    e2 = jnp.exp(v2 - v1)
    denom = 1.0 + e2
    w1 = (1.0 / denom) * g_pw
    w2 = (e2 / denom) * g_pw
    id1 = g_sel * float(MOE_EXPERTS_PER_GROUP) + i1
    id2 = g_sel * float(MOE_EXPERTS_PER_GROUP) + i2
    erow = lax.broadcasted_iota(jnp.int32, (LANES, tm), 0).astype(F32)
    comb_t = jnp.where(erow == id1, w1, 0.0) + jnp.where(erow == id2, w2, 0.0)
    comb_ref[...] = comb_t.T


def _router_weights(w_rg, b_rg, w_re, b_re):
    S = ROUTER_SLOT
    w = jnp.zeros((D_MODEL, LANES), F32).at[:, 0:MOE_GROUPS].set(w_rg)
    b = jnp.zeros((LANES,), F32).at[0:MOE_GROUPS].set(b_rg)
    for g in range(MOE_GROUPS):
        src = slice(g * MOE_EXPERTS_PER_GROUP, (g + 1) * MOE_EXPERTS_PER_GROUP)
        dst = slice((1 + g) * S, (1 + g) * S + MOE_EXPERTS_PER_GROUP)
        w = w.at[:, dst].set(w_re[:, src])
        b = b.at[dst].set(b_re[src])
    hi = w.astype(BF16)
    lo = (w - hi.astype(F32)).astype(BF16)
    return jnp.stack([hi, lo]), b.reshape(1, LANES)


def _merge(x2d, attn, ssd_out, gates, w_out_a, w_out_b, w_out, ffn_norm_g, wr, br, tm=256):
    t = x2d.shape[0]
    row = lambda i: (i, 0)
    return pl.pallas_call(
        _merge_kernel,
        grid=(t // tm,),
        in_specs=[pl.BlockSpec((tm, D_MODEL), row),
                  pl.BlockSpec((tm, A_WIDTH), row),
                  pl.BlockSpec((tm, SSD_D_INNER), row),
                  pl.BlockSpec((tm, 2 * D_MODEL), row),
                  _const_spec((A_WIDTH, D_MODEL)),
                  _const_spec((SSD_D_INNER, D_MODEL)),
                  _const_spec((D_MODEL, D_MODEL)),
                  _const_spec((1, D_MODEL)),
                  _const_spec((2, D_MODEL, LANES)),
                  _const_spec((1, LANES))],
        out_specs=[pl.BlockSpec((tm, D_MODEL), row),
                   pl.BlockSpec((tm, D_MODEL), row),
                   pl.BlockSpec((tm, LANES), row)],
        out_shape=[jax.ShapeDtypeStruct((t, D_MODEL), F32),
                   jax.ShapeDtypeStruct((t, D_MODEL), BF16),
                   jax.ShapeDtypeStruct((t, LANES), F32)],
        compiler_params=_params(("parallel",)),
        name="merge_router",
    )(x2d, attn, ssd_out, gates, w_out_a.astype(BF16), w_out_b.astype(BF16),
      w_out.astype(BF16), ffn_norm_g.reshape(1, -1), wr, br)


def _moe_kernel(h_ref, comb_ref, wg_ref, wu_ref, wd_ref, o_ref, acc_ref):
    e = pl.program_id(1)

    @pl.when(e == 0)
    def _():
        acc_ref[...] = jnp.zeros_like(acc_ref)

    h = h_ref[...]
    he = _silu(_dot(h, wg_ref[0])) * _dot(h, wu_ref[0])
    comb = comb_ref[...]
    lane = lax.broadcasted_iota(jnp.int32, comb.shape, 1)
    ce = jnp.sum(jnp.where(lane == e, comb, 0.0), axis=1, keepdims=True)
    acc_ref[...] += _dot((he * ce).astype(BF16), wd_ref[0])

    @pl.when(e == MOE_EXPERTS - 1)
    def _():
        o_ref[...] = acc_ref[...]


def _moe(h2, comb, w_gate, w_up, w_down, tm=1024):
    t = h2.shape[0]
    return pl.pallas_call(
        _moe_kernel,
        grid=(t // tm, MOE_EXPERTS),
        in_specs=[pl.BlockSpec((tm, D_MODEL), lambda i, e: (i, 0)),
                  pl.BlockSpec((tm, LANES), lambda i, e: (i, 0)),
                  pl.BlockSpec((1, D_MODEL, MOE_D_FF), lambda i, e: (e, 0, 0)),
                  pl.BlockSpec((1, D_MODEL, MOE_D_FF), lambda i, e: (e, 0, 0)),
                  pl.BlockSpec((1, MOE_D_FF, D_MODEL), lambda i, e: (e, 0, 0))],
        out_specs=pl.BlockSpec((tm, D_MODEL), lambda i, e: (i, 0)),
        out_shape=jax.ShapeDtypeStruct((t, D_MODEL), F32),
        scratch_shapes=[pltpu.VMEM((tm, D_MODEL), F32)],
        compiler_params=_params(("parallel", "arbitrary")),
        name="moe",
    )(h2, comb, w_gate.astype(BF16), w_up.astype(BF16), w_down.astype(BF16))


def _final_kernel(x1_ref, moe_ref, p_ref, wpg_ref, wpp_ref, g_ref, o_ref):
    x2 = x1_ref[...] + moe_ref[...]
    gate = _sigmoid(_dot(x2.astype(BF16), wpg_ref[...]))
    emb = _dot(p_ref[...].astype(BF16), wpp_ref[...])
    o_ref[...] = _rms(x2 + gate * emb, g_ref[...])


def _final(x1, moe_out, p2d, w_ple_gate, w_ple_proj, final_norm_g, tm=512):
    t = x1.shape[0]
    row = lambda i: (i, 0)
    return pl.pallas_call(
        _final_kernel,
        grid=(t // tm,),
        in_specs=[pl.BlockSpec((tm, D_MODEL), row),
                  pl.BlockSpec((tm, D_MODEL), row),
                  pl.BlockSpec((tm, PLE_DIM), row),
                  _const_spec((D_MODEL, D_MODEL)),
                  _const_spec((PLE_DIM, D_MODEL)),
                  _const_spec((1, D_MODEL))],
        out_specs=pl.BlockSpec((tm, D_MODEL), row),
        out_shape=jax.ShapeDtypeStruct((t, D_MODEL), F32),
        compiler_params=_params(("parallel",)),
        name="final",
    )(x1, moe_out, p2d, w_ple_gate.astype(BF16), w_ple_proj.astype(BF16),
      final_norm_g.reshape(1, -1))


def kernel(x, p, mix_norm_g, w_in, conv_w, conv_b, dt_bias, a_log, d_skip, ssd_norm_g, w_out_a, w_out_b, w_out, ffn_norm_g, w_rg, b_rg, w_re, b_re, w_gate, w_up, w_down, w_ple_proj, w_ple_gate, final_norm_g):
    bsz, seq, _ = x.shape
    assert w_in.shape[0] == 1, "single-layer block only"
    t = bsz * seq
    xr = x.reshape(t, D_MODEL)
    for i in range(1):
        qkv, z, xbc, dt_raw, gates = _in_proj(xr, mix_norm_g[i], w_in[i])
        attn = _moba(qkv, bsz, seq)
        ssd_out = _ssd(xbc, dt_raw, z, conv_w[i], conv_b[i], dt_bias[i], a_log[i],
                       d_skip[i], ssd_norm_g[i], bsz, seq)
        wr, br = _router_weights(w_rg[i], b_rg[i], w_re[i], b_re[i])
        x1, h2, comb = _merge(xr, attn, ssd_out, gates, w_out_a[i], w_out_b[i], w_out[i],
                              ffn_norm_g[i], wr, br)
        moe_out = _moe(h2, comb, w_gate[i], w_up[i], w_down[i])
        xr = _final(x1, moe_out, p[i].reshape(t, PLE_DIM), w_ple_gate[i], w_ple_proj[i],
                    final_norm_g)
    return xr.reshape(bsz, seq, D_MODEL)
```

```python
import functools

import jax
import jax.numpy as jnp
from jax import lax
from jax.experimental import pallas as pl
from jax.experimental.pallas import tpu as pltpu

F32 = jnp.float32
BF16 = jnp.bfloat16
EPS = 1e-6

D_MODEL = 1024
PLE_DIM = 256

A_HEADS = 8
A_HEAD_DIM = 64
A_WIDTH = A_HEADS * A_HEAD_DIM
MOBA_BLOCK = 256
MOBA_TOPK = 3

SSD_D_INNER = 1024
SSD_HEAD_DIM = 64
SSD_HEADS = 16
SSD_GROUPS = 2
SSD_D_STATE = 128
SSD_CONV = 4
SSD_CHUNK = 256
SSD_CONV_DIM = SSD_D_INNER + 2 * SSD_GROUPS * SSD_D_STATE
SSD_GROUP_WIDTH = SSD_D_INNER // SSD_GROUPS
SSD_HEADS_PER_GROUP = SSD_HEADS // SSD_GROUPS

MOE_GROUPS = 4
MOE_EXPERTS_PER_GROUP = 4
MOE_EXPERTS = 16
MOE_D_FF = 512
MOE_PAIRS = 6
MOE_CLASSES = MOE_GROUPS * MOE_PAIRS
MOE_ROW = D_MODEL + 128

OFF_Q = 0
OFF_Z = 3 * A_WIDTH
OFF_XBC = OFF_Z + SSD_D_INNER
OFF_DT = OFF_XBC + SSD_CONV_DIM
OFF_GA = OFF_DT + SSD_HEADS
IN_DIM = OFF_GA + 2 * D_MODEL

LANES = 128
SUBLANES = 8
ROUTER_SLOT = 8
VMEM_LIMIT = 56 * 1024 * 1024

NEG_INF = float("-inf")


def _dot(a, b):
    return jnp.dot(a, b, preferred_element_type=F32)


def _dot_nt(a, b):
    return lax.dot_general(a, b, (((1,), (1,)), ((), ())), preferred_element_type=F32)


def _split3(x):
    hi = x.astype(BF16)
    r1 = x - hi.astype(F32)
    mid = r1.astype(BF16)
    lo = (r1 - mid.astype(F32)).astype(BF16)
    return hi, mid, lo


def _sigmoid(x):
    return 1.0 / (1.0 + jnp.exp(-x))


def _silu(x):
    return x * _sigmoid(x)


def _rms(x, g):
    ms = jnp.mean(x * x, axis=-1, keepdims=True)
    return x * lax.rsqrt(ms + EPS) * g


def _const_spec(shape):
    nd = len(shape)
    return pl.BlockSpec(shape, lambda *_: (0,) * nd)


def _params(sem):
    return pltpu.CompilerParams(dimension_semantics=sem, vmem_limit_bytes=VMEM_LIMIT)


def _inproj_kernel(x_ref, g_ref, wqkv_ref, wz_ref, wxbc_ref, wdt_ref, wg_ref,
                   qkv_ref, z_ref, xbc_ref, dt_ref, gates_ref):
    h = _rms(x_ref[...], g_ref[...]).astype(BF16)
    qkv_ref[...] = _dot(h, wqkv_ref[...]).astype(BF16)
    z_ref[...] = _dot(h, wz_ref[...]).astype(BF16)
    xbc_ref[...] = _dot(h, wxbc_ref[...]).astype(BF16)
    dt_ref[...] = _dot(h, wdt_ref[...])
    gates_ref[...] = _dot(h, wg_ref[...]).astype(BF16)


def _in_proj(x2d, g, w_in, tm=256):
    t = x2d.shape[0]
    wb = w_in.astype(BF16)
    wqkv = wb[:, OFF_Q:OFF_Z]
    wz = wb[:, OFF_Z:OFF_XBC]
    wxbc = wb[:, OFF_XBC:OFF_DT]
    wdt = jnp.pad(wb[:, OFF_DT:OFF_GA], ((0, 0), (0, LANES - SSD_HEADS)))
    wg = wb[:, OFF_GA:IN_DIM]
    widths = (3 * A_WIDTH, SSD_D_INNER, SSD_CONV_DIM, LANES, 2 * D_MODEL)
    dtypes = (BF16, BF16, BF16, F32, BF16)
    return pl.pallas_call(
        _inproj_kernel,
        grid=(t // tm,),
        in_specs=[pl.BlockSpec((tm, D_MODEL), lambda i: (i, 0)),
                  _const_spec((1, D_MODEL))]
                 + [_const_spec((D_MODEL, w)) for w in widths],
        out_specs=[pl.BlockSpec((tm, w), lambda i: (i, 0)) for w in widths],
        out_shape=[jax.ShapeDtypeStruct((t, w), d) for w, d in zip(widths, dtypes)],
        compiler_params=_params(("parallel",)),
        name="in_proj",
    )(x2d, g.reshape(1, D_MODEL), wqkv, wz, wxbc, wdt, wg)


MOBA_GATE_ROWS = 32
MOBA_K_LANES = 2 * LANES


def _moba_kernel(slopes_ref, q_ref, k_ref, v_ref, o_ref, vt_ref, kc_ref, s_ref):
    hp = pl.program_id(1)
    seq = k_ref.shape[0]
    nb = seq // MOBA_BLOCK
    blk = MOBA_BLOCK
    hd = A_HEAD_DIM
    g0 = MOBA_GATE_ROWS
    slopes = [slopes_ref[2 * hp + e] for e in range(2)]

    for j in range(nb):
        vt_ref[j] = v_ref[j * blk:(j + 1) * blk, :].astype(F32).T.astype(BF16)
    kf = k_ref[...].astype(F32).reshape(nb, blk, LANES)
    kmean = jnp.sum(kf, axis=1) * (1.0 / blk)
    km_terms = [t.astype(F32) for t in _split3(kmean)] + [jnp.zeros((g0 - 3 * nb, LANES), F32)]
    kc_ref[0:g0, 0:LANES] = jnp.concatenate(km_terms, axis=0).astype(BF16)
    kc_ref[0:g0, LANES:MOBA_K_LANES] = jnp.zeros((g0, LANES), BF16)
    kc_ref[g0:g0 + seq, 0:LANES] = k_ref[...]
    row = lax.broadcasted_iota(jnp.int32, (seq, LANES), 0)
    alane = lax.broadcasted_iota(jnp.int32, (seq, LANES), 1)
    k_local = jnp.bitwise_and(row, blk - 1).astype(F32)
    aug = jnp.zeros((seq, LANES), F32)
    for e in range(2):
        val = k_local * slopes[e]
        hi = val.astype(BF16).astype(F32)
        aug = jnp.where(alane == e, hi, jnp.where(alane == 2 + e, val - hi, aug))
    kc_ref[g0:g0 + seq, LANES:MOBA_K_LANES] = aug.astype(BF16)

    qlane = lax.broadcasted_iota(jnp.int32, (blk, LANES), 1)
    kpos = lax.broadcasted_iota(jnp.int32, (blk, 2 * blk), 0)
    qpos = jnp.bitwise_and(lax.broadcasted_iota(jnp.int32, (blk, 2 * blk), 1), blk - 1)
    causal = qpos >= kpos
    q_local = lax.broadcasted_iota(jnp.int32, (1, blk), 1).astype(F32)
    blk_row = lax.broadcasted_iota(jnp.int32, (nb, blk), 0)

    def scores(i):
        q2 = q_ref[i * blk:(i + 1) * blk, :]
        halves = []
        for e in range(2):
            in_head = (qlane >= e * hd) & (qlane < (e + 1) * hd)
            left = jnp.where(in_head, q2.astype(F32) * (hd ** -0.5), 0.0).astype(BF16)
            right = jnp.where((qlane == e) | (qlane == 2 + e), 1.0, 0.0).astype(BF16)
            halves.append(jnp.concatenate([left, right], axis=1))
        q_both = jnp.concatenate(halves, axis=0)
        nk = (i + 1) * blk
        s_ext = _dot_nt(kc_ref[0:g0 + nk, :], q_both)
        slot0 = i * (i + 1) // 2
        for j in range(i + 1):
            sj = s_ext[g0 + j * blk:g0 + (j + 1) * blk, :]
            if j == i:
                sj = jnp.where(causal, sj, NEG_INF)
            s_ref[slot0 + j] = sj
        past = blk_row < i
        shifts = []
        for e in range(2):
            cols = slice(e * blk, (e + 1) * blk)
            gates = s_ext[0:nb, cols] + s_ext[nb:2 * nb, cols] + s_ext[2 * nb:3 * nb, cols]
            row_bias = []
            for j in range(i + 1):
                rb = (q_local + float((i - j) * blk)) * (-slopes[e])
                if j < i:
                    gj = gates[j:j + 1, :]
                    beats = past & ((gates > gj) | ((gates == gj) & (blk_row < j)))
                    rank = jnp.sum(beats.astype(F32), axis=0, keepdims=True)
                    rb = rb + jnp.where(rank < float(MOBA_TOPK), 0.0, NEG_INF)
                row_bias.append(rb)
            m = None
            for j in range(i + 1):
                bm = jnp.max(s_ref[slot0 + j][:, cols], axis=0, keepdims=True) + row_bias[j]
                m = bm if m is None else jnp.maximum(m, bm)
            shifts.append([rb - m for rb in row_bias])
        return shifts

    def outputs(i, shifts):
        slot0 = i * (i + 1) // 2
        outs = []
        for e in range(2):
            cols = slice(e * blk, (e + 1) * blk)
            l = jnp.zeros((1, blk), F32)
            acc = jnp.zeros((hd, blk), F32)
            for j in range(i + 1):
                p = jnp.exp(s_ref[slot0 + j][:, cols] + shifts[e][j])
                l = l + jnp.sum(p, axis=0, keepdims=True)
                acc = acc + _dot(vt_ref[j][e * hd:(e + 1) * hd, :], p.astype(BF16))
            outs.append(acc / l)
        o_ref[i * blk:(i + 1) * blk, :] = jnp.concatenate(outs, axis=0).T.astype(BF16)

    shifts = scores(0)
    for i in range(nb):
        nxt = scores(i + 1) if i + 1 < nb else None
        outputs(i, shifts)
        shifts = nxt


def _moba(qkv, bsz, seq):
    t = bsz * seq
    nb = seq // MOBA_BLOCK
    n_pairs = A_WIDTH // LANES
    slopes = 2.0 ** (-8.0 * jnp.arange(1, A_HEADS + 1, dtype=F32) / A_HEADS)
    kv_off = A_WIDTH // LANES
    return pl.pallas_call(
        _moba_kernel,
        grid=(bsz, n_pairs),
        in_specs=[pl.BlockSpec(memory_space=pltpu.SMEM),
                  pl.BlockSpec((seq, LANES), lambda b, h: (b, h)),
                  pl.BlockSpec((seq, LANES), lambda b, h: (b, kv_off + h)),
                  pl.BlockSpec((seq, LANES), lambda b, h: (b, 2 * kv_off + h))],
        out_specs=pl.BlockSpec((seq, LANES), lambda b, h: (b, h)),
        out_shape=jax.ShapeDtypeStruct((t, A_WIDTH), BF16),
        scratch_shapes=[pltpu.VMEM((nb, LANES, MOBA_BLOCK), BF16),
                        pltpu.VMEM((MOBA_GATE_ROWS + seq, MOBA_K_LANES), BF16),
                        pltpu.VMEM((nb * (nb + 1) // 2, MOBA_BLOCK, 2 * MOBA_BLOCK), F32)],
        compiler_params=_params(("parallel", "parallel")),
        name="moba",
    )(slopes, qkv, qkv, qkv)


def _ssd_kernel(xbc_ref, dt_ref, z_ref, cw_ref, cb_ref, dtb_ref, alog_ref, dskip_ref, ng_ref,
                expand_ref, o_ref, xe_ref, state_ref):
    c = pl.program_id(1)
    L = SSD_CHUNK
    tail = SUBLANES

    @pl.when(c == 0)
    def _():
        xe_ref[0:tail, :] = jnp.zeros((tail, SSD_CONV_DIM), F32)
        state_ref[...] = jnp.zeros_like(state_ref)

    @pl.when(c > 0)
    def _():
        xe_ref[0:tail, :] = xe_ref[L:L + tail, :]

    xe_ref[tail:tail + L, :] = xbc_ref[...].astype(F32)

    first = tail - (SSD_CONV - 1)
    conv = cb_ref[...] + cw_ref[0:1, :] * xe_ref[first:first + L, :]
    for w in range(1, SSD_CONV):
        conv = conv + cw_ref[w:w + 1, :] * xe_ref[first + w:first + w + L, :]
    act = _silu(conv)
    xs = act[:, :SSD_D_INNER]
    gs = SSD_GROUPS * SSD_D_STATE

    lane = lax.broadcasted_iota(jnp.int32, (L, LANES), 1)
    head_lane = lane < SSD_HEADS
    xdt_in = dt_ref[...] + dtb_ref[...]
    dt = jnp.maximum(xdt_in, 0.0) + jnp.log1p(jnp.exp(-jnp.abs(xdt_in)))
    dt = jnp.where(head_lane, dt, 0.0)
    a = -jnp.exp(alog_ref[...])
    da = dt * a

    r = lax.broadcasted_iota(jnp.int32, (L, L), 0)
    col = lax.broadcasted_iota(jnp.int32, (L, L), 1)
    tril = r >= col
    tri = jnp.where(tril, 1.0, 0.0).astype(BF16)
    d_hi, d_mid, d_lo = _split3(da)
    cs = _dot(tri, d_hi) + _dot(tri, d_mid) + _dot(tri, d_lo)
    cs_t = cs.T
    cs_last = cs[L - 1:L, :]

    def expand(f):
        hi = f.astype(BF16)
        lo = (f - hi.astype(F32)).astype(BF16)
        return _dot(hi, expand_ref[...]) + _dot(lo, expand_ref[...])

    dt_x = expand(dt)
    to_end_x = expand(jnp.where(head_lane, jnp.exp(cs_last - cs), 0.0))
    from_start_x = expand(jnp.where(head_lane, jnp.exp(cs), 0.0))

    xdt = xs * dt_x
    xdt_b = xdt.astype(BF16)
    xdte_b = (xdt * to_end_x).astype(BF16)

    lane_c = lax.broadcasted_iota(jnp.int32, (L, LANES), 1)
    y_parts = []
    for g in range(SSD_GROUPS):
        bm = act[:, SSD_D_INNER + g * SSD_D_STATE:SSD_D_INNER + (g + 1) * SSD_D_STATE]
        cm = act[:, SSD_D_INNER + gs + g * SSD_D_STATE:SSD_D_INNER + gs + (g + 1) * SSD_D_STATE]
        bm_b = bm.astype(BF16)
        cm_b = cm.astype(BF16)
        gmat = _dot_nt(cm_b, bm_b)
        st_prev = state_ref[g]
        gw = slice(g * SSD_GROUP_WIDTH, (g + 1) * SSD_GROUP_WIDTH)
        y_off = _dot(cm_b, st_prev.astype(BF16)) * from_start_x[:, gw]
        decay = from_start_x[L - 1:L, gw]
        state_ref[g] = st_prev * decay + _dot(bm.T.astype(BF16), xdte_b[:, gw])
        for pr in range(SSD_HEADS_PER_GROUP // 2):
            colbase = g * SSD_GROUP_WIDTH + pr * LANES
            xcol = xdt_b[:, colbase:colbase + LANES]
            res = []
            for e in range(2):
                h = g * SSD_HEADS_PER_GROUP + pr * 2 + e
                diff = cs[:, h:h + 1] - cs_t[h:h + 1, :]
                lmat = jnp.exp(jnp.where(tril, diff, NEG_INF))
                res.append(_dot((gmat * lmat).astype(BF16), xcol))
            y_d = jnp.where(lane_c < SSD_HEAD_DIM, res[0], res[1])
            y_parts.append(y_d + y_off[:, pr * LANES:(pr + 1) * LANES])
    y = jnp.concatenate(y_parts, axis=1) + dskip_ref[...] * xs

    u = y * _silu(z_ref[...].astype(F32))
    outs = []
    for g in range(SSD_GROUPS):
        gw = slice(g * SSD_GROUP_WIDTH, (g + 1) * SSD_GROUP_WIDTH)
        outs.append(_rms(u[:, gw], ng_ref[:, gw]))
    o_ref[...] = jnp.concatenate(outs, axis=1).astype(BF16)


def _ssd(xbc, dt_raw, z, conv_w, conv_b, dt_bias, a_log, d_skip, norm_g, bsz, seq):
    t = bsz * seq
    nc = seq // SSD_CHUNK
    L = SSD_CHUNK
    pad = LANES - SSD_HEADS
    dtb = jnp.pad(dt_bias, (0, pad)).reshape(1, LANES)
    alog = jnp.pad(a_log, (0, pad)).reshape(1, LANES)
    dskip = jnp.repeat(d_skip, SSD_HEAD_DIM).reshape(1, SSD_D_INNER)
    head_of_chan = jnp.arange(SSD_D_INNER) // SSD_HEAD_DIM
    expand = (jnp.arange(LANES)[:, None] == head_of_chan[None, :]).astype(BF16)
    row = lambda b, c: (b * nc + c, 0)
    return pl.pallas_call(
        _ssd_kernel,
        grid=(bsz, nc),
        in_specs=[pl.BlockSpec((L, SSD_CONV_DIM), row),
                  pl.BlockSpec((L, LANES), row),
                  pl.BlockSpec((L, SSD_D_INNER), row),
                  _const_spec((SSD_CONV, SSD_CONV_DIM)),
                  _const_spec((1, SSD_CONV_DIM)),
                  _const_spec((1, LANES)),
                  _const_spec((1, LANES)),
                  _const_spec((1, SSD_D_INNER)),
                  _const_spec((1, SSD_D_INNER)),
                  _const_spec((LANES, SSD_D_INNER))],
        out_specs=pl.BlockSpec((L, SSD_D_INNER), row),
        out_shape=jax.ShapeDtypeStruct((t, SSD_D_INNER), BF16),
        scratch_shapes=[pltpu.VMEM((L + 2 * SUBLANES, SSD_CONV_DIM), F32),
                        pltpu.VMEM((SSD_GROUPS, SSD_D_STATE, SSD_GROUP_WIDTH), F32)],
        compiler_params=_params(("arbitrary", "arbitrary")),
        name="ssd",
    )(xbc, dt_raw, z, conv_w, conv_b.reshape(1, -1), dtb, alog, dskip,
      norm_g.reshape(1, -1), expand)


def _first_max(vals, rows, n_rows):
    vmax = jnp.max(vals, axis=0, keepdims=True)
    idx = jnp.min(jnp.where(vals == vmax, rows, float(n_rows)), axis=0, keepdims=True)
    return vmax, idx


def _merge_kernel(x_ref, attn_ref, ssd_ref, gates_ref, woa_ref, wob_ref, wo_ref, ng_ref,
                  wr_ref, br_ref, x1e_ref, route_ref):
    tm = x_ref.shape[0]
    y_a = _dot(attn_ref[...], woa_ref[...])
    y_b = _dot(ssd_ref[...], wob_ref[...])
    gates = gates_ref[...].astype(F32)
    mix = _sigmoid(gates[:, :D_MODEL]) * y_a + _sigmoid(gates[:, D_MODEL:]) * y_b
    x1 = x_ref[...] + _dot(mix.astype(BF16), wo_ref[...])
    x1e_ref[:, 0:D_MODEL] = x1
    h2 = _rms(x1, ng_ref[...])

    h_hi = h2.astype(BF16)
    h_lo = (h2 - h_hi.astype(F32)).astype(BF16)
    logits = (_dot(h_hi, wr_ref[0]) + _dot(h_lo, wr_ref[0]) + _dot(h_hi, wr_ref[1])
              + br_ref[...])
    lt = logits.T
    S = ROUTER_SLOT
    rows = lax.broadcasted_iota(jnp.int32, (S, tm), 0).astype(F32)
    real = rows < float(MOE_GROUPS)
    gl = jnp.where(real, lt[0:S, :], NEG_INF)
    gmax = jnp.max(gl, axis=0, keepdims=True)
    ge = jnp.exp(gl - gmax)
    gprob = ge / jnp.sum(ge, axis=0, keepdims=True)
    g_pw, g_sel = _first_max(gprob, rows, S)
    el = jnp.zeros((S, tm), F32)
    for g in range(MOE_GROUPS):
        el = jnp.where(g_sel == float(g), lt[(1 + g) * S:(2 + g) * S, :], el)
    el = jnp.where(real, el, NEG_INF)
    v1, i1 = _first_max(el, rows, S)
    el2 = jnp.where(rows == i1, NEG_INF, el)
    v2, i2 = _first_max(el2, rows, S)
    e2 = jnp.exp(v2 - v1)
    denom = 1.0 + e2
    w1 = (1.0 / denom) * g_pw
    w2 = (e2 / denom) * g_pw
    lo = jnp.minimum(i1, i2)
    hi = jnp.maximum(i1, i2)
    w_lo = jnp.where(i1 < i2, w1, w2)
    w_hi = jnp.where(i1 < i2, w2, w1)
    pair = lo * (7.0 - lo) * 0.5 + (hi - lo - 1.0)
    cls = g_sel * float(MOE_PAIRS) + pair
    route = jnp.concatenate([w_lo, w_hi, cls, jnp.zeros((S - 3, tm), F32)], axis=0)
    route_ref[...] = route
    route_pad = jnp.concatenate([route, jnp.zeros((LANES - S, tm), F32)], axis=0)
    x1e_ref[:, D_MODEL:D_MODEL + LANES] = route_pad.T


def _router_weights(w_rg, b_rg, w_re, b_re):
    S = ROUTER_SLOT
    w = jnp.zeros((D_MODEL, LANES), F32).at[:, 0:MOE_GROUPS].set(w_rg)
    b = jnp.zeros((LANES,), F32).at[0:MOE_GROUPS].set(b_rg)
    for g in range(MOE_GROUPS):
        src = slice(g * MOE_EXPERTS_PER_GROUP, (g + 1) * MOE_EXPERTS_PER_GROUP)
        dst = slice((1 + g) * S, (1 + g) * S + MOE_EXPERTS_PER_GROUP)
        w = w.at[:, dst].set(w_re[:, src])
        b = b.at[dst].set(b_re[src])
    hi = w.astype(BF16)
    lo = (w - hi.astype(F32)).astype(BF16)
    return jnp.stack([hi, lo]), b.reshape(1, LANES)


def _merge(x2d, attn, ssd_out, gates, w_out_a, w_out_b, w_out, ffn_norm_g, wr, br, tm=256):
    t = x2d.shape[0]
    row = lambda i: (i, 0)
    return pl.pallas_call(
        _merge_kernel,
        grid=(t // tm,),
        in_specs=[pl.BlockSpec((tm, D_MODEL), row),
                  pl.BlockSpec((tm, A_WIDTH), row),
                  pl.BlockSpec((tm, SSD_D_INNER), row),
                  pl.BlockSpec((tm, 2 * D_MODEL), row),
                  _const_spec((A_WIDTH, D_MODEL)),
                  _const_spec((SSD_D_INNER, D_MODEL)),
                  _const_spec((D_MODEL, D_MODEL)),
                  _const_spec((1, D_MODEL)),
                  _const_spec((2, D_MODEL, LANES)),
                  _const_spec((1, LANES))],
        out_specs=[pl.BlockSpec((tm, MOE_ROW), row),
                   pl.BlockSpec((ROUTER_SLOT, tm), lambda i: (0, i))],
        out_shape=[jax.ShapeDtypeStruct((t, MOE_ROW), F32),
                   jax.ShapeDtypeStruct((ROUTER_SLOT, t), F32)],
        compiler_params=_params(("parallel",)),
        name="merge_router",
    )(x2d, attn, ssd_out, gates, w_out_a.astype(BF16), w_out_b.astype(BF16),
      w_out.astype(BF16), ffn_norm_g.reshape(1, -1), wr, br)


MOE_TILE = 256


def _dispatch_plan(cls, t, tm, nt):
    i32 = jnp.int32
    order = jnp.argsort(cls, stable=True).astype(i32)
    counts = jnp.sum((cls[:, None] == jnp.arange(MOE_CLASSES, dtype=i32)[None, :]).astype(i32),
                     axis=0)
    tiles = (counts + tm - 1) // tm
    tile_end = jnp.cumsum(tiles)
    tile_start = tile_end - tiles
    total = tile_end[-1]
    offset = jnp.cumsum(counts) - counts
    j = jnp.arange(nt, dtype=i32)
    jc = jnp.minimum(j, total - 1)
    c = jnp.minimum(jnp.searchsorted(tile_end, jc, side="right").astype(i32), MOE_CLASSES - 1)
    r = jc - tile_start[c]
    nvalid = jnp.clip(counts[c] - r * tm, 0, tm)
    lane = jnp.arange(tm, dtype=i32)[None, :]
    slot = (offset[c] + r * tm)[:, None] + lane
    valid = lane < nvalid[:, None]
    tok = order[jnp.clip(slot, 0, t - 1)]
    gidx = jnp.where(valid, tok, 0)
    sidx = jnp.where(valid, tok, t + (j % 2)[:, None] * tm + lane)
    grp = c // MOE_PAIRS
    pair = c % MOE_PAIRS
    first = jnp.array([0, 0, 0, 1, 1, 2], i32)[pair]
    second = jnp.array([1, 2, 3, 2, 3, 3], i32)[pair]
    ea = grp * MOE_EXPERTS_PER_GROUP + first
    eb = grp * MOE_EXPERTS_PER_GROUP + second
    return ea, eb, total.reshape(1), gidx.reshape(nt, 1, tm), sidx.reshape(nt, 1, tm)


def _moe_kernel(ea_ref, eb_ref, tot_ref, g0_ref, gn_ref, s_ref, x_hbm, ng_ref,
                wga_ref, wua_ref, wda_ref, wgb_ref, wub_ref, wdb_ref, o_hbm,
                xbuf, obuf, sem_g, sem_s):
    j = pl.program_id(0)
    total = tot_ref[0]
    tm = xbuf.shape[1]
    cur = lax.rem(j, 2)
    nxt = 1 - cur

    def gather_start(idx_ref, slot):
        def body(i, carry):
            tok = idx_ref[0, 0, i]
            pltpu.make_async_copy(x_hbm.at[pl.ds(tok, 1), :], xbuf.at[slot, pl.ds(i, 1), :],
                                  sem_g.at[slot]).start()
            return carry
        lax.fori_loop(0, tm, body, 0, unroll=8)

    def gather_wait(slot):
        pltpu.make_async_copy(xbuf.at[slot], xbuf.at[slot], sem_g.at[slot]).wait()

    def scatter_start(slot):
        def body(i, carry):
            tok = s_ref[0, 0, i]
            pltpu.make_async_copy(obuf.at[slot, pl.ds(i, 1), :], o_hbm.at[pl.ds(tok, 1), :],
                                  sem_s.at[slot]).start()
            return carry
        lax.fori_loop(0, tm, body, 0, unroll=8)

    def scatter_wait(slot):
        pltpu.make_async_copy(obuf.at[slot], obuf.at[slot], sem_s.at[slot]).wait()

    @pl.when(j == 0)
    def _():
        gather_start(g0_ref, 0)

    @pl.when((j >= 2) & (j <= total))
    def _():
        scatter_wait(cur)

    @pl.when(j < total)
    def _():
        gather_wait(cur)
        gather_start(gn_ref, nxt)
        xg = xbuf[cur]
        x1 = xg[:, 0:D_MODEL]
        w_lo = xg[:, D_MODEL:D_MODEL + 1]
        w_hi = xg[:, D_MODEL + 1:D_MODEL + 2]
        h = _rms(x1, ng_ref[...]).astype(BF16)
        he_a = _silu(_dot(h, wga_ref[0])) * _dot(h, wua_ref[0]) * w_lo
        he_b = _silu(_dot(h, wgb_ref[0])) * _dot(h, wub_ref[0]) * w_hi
        y = _dot(he_a.astype(BF16), wda_ref[0]) + _dot(he_b.astype(BF16), wdb_ref[0])
        obuf[cur] = x1 + y
        scatter_start(cur)

    @pl.when(j == total)
    def _():
        gather_wait(cur)
        scatter_wait(nxt)
        n_tok = o_hbm.shape[0] - 2 * tm
        obuf[0] = jnp.zeros((tm, D_MODEL), F32)
        for part in range(2):
            spare = pltpu.make_async_copy(obuf.at[0], o_hbm.at[pl.ds(n_tok + part * tm, tm), :],
                                          sem_s.at[0])
            spare.start()
            spare.wait()


def _moe(x1e, cls, ffn_norm_g, w_gate, w_up, w_down, tm=MOE_TILE):
    t = x1e.shape[0]
    nt = t // tm + MOE_CLASSES + 1
    ea, eb, total, gidx, sidx = _dispatch_plan(cls, t, tm, nt)
    idx_spec = lambda f: pl.BlockSpec((1, 1, tm), f, memory_space=pltpu.SMEM)
    wspec = lambda shape, which: pl.BlockSpec(
        (1,) + shape, lambda j, ea, eb, tot: ((ea, eb)[which][j], 0, 0))
    wg, wu, wd = w_gate.astype(BF16), w_up.astype(BF16), w_down.astype(BF16)
    grid_spec = pltpu.PrefetchScalarGridSpec(
        num_scalar_prefetch=3,
        grid=(nt,),
        in_specs=[idx_spec(lambda j, ea, eb, tot: (0, 0, 0)),
                  idx_spec(lambda j, ea, eb, tot: (jnp.minimum(j + 1, tot[0] - 1), 0, 0)),
                  idx_spec(lambda j, ea, eb, tot: (j, 0, 0)),
                  pl.BlockSpec(memory_space=pl.ANY),
                  pl.BlockSpec((1, D_MODEL), lambda j, ea, eb, tot: (0, 0)),
                  wspec((D_MODEL, MOE_D_FF), 0), wspec((D_MODEL, MOE_D_FF), 0),
                  wspec((MOE_D_FF, D_MODEL), 0),
                  wspec((D_MODEL, MOE_D_FF), 1), wspec((D_MODEL, MOE_D_FF), 1),
                  wspec((MOE_D_FF, D_MODEL), 1)],
        out_specs=pl.BlockSpec(memory_space=pl.ANY),
        scratch_shapes=[pltpu.VMEM((2, tm, MOE_ROW), F32),
                        pltpu.VMEM((2, tm, D_MODEL), F32),
                        pltpu.SemaphoreType.DMA((2,)),
                        pltpu.SemaphoreType.DMA((2,))])
    return pl.pallas_call(
        _moe_kernel,
        grid_spec=grid_spec,
        out_shape=jax.ShapeDtypeStruct((t + 2 * tm, D_MODEL), F32),
        compiler_params=_params(("arbitrary",)),
        name="moe",
    )(ea, eb, total, gidx, gidx, sidx, x1e, ffn_norm_g.reshape(1, -1), wg, wu, wd, wg, wu, wd)


def _final_kernel(x2_ref, p_ref, wpg_ref, wpp_ref, g_ref, o_ref):
    x2 = x2_ref[...]
    gate = _sigmoid(_dot(x2.astype(BF16), wpg_ref[...]))
    emb = _dot(p_ref[...].astype(BF16), wpp_ref[...])
    o_ref[...] = _rms(x2 + gate * emb, g_ref[...])


def _final(x2, p2d, w_ple_gate, w_ple_proj, final_norm_g, tm=512):
    t = p2d.shape[0]
    row = lambda i: (i, 0)
    return pl.pallas_call(
        _final_kernel,
        grid=(t // tm,),
        in_specs=[pl.BlockSpec((tm, D_MODEL), row),
                  pl.BlockSpec((tm, PLE_DIM), row),
                  _const_spec((D_MODEL, D_MODEL)),
                  _const_spec((PLE_DIM, D_MODEL)),
                  _const_spec((1, D_MODEL))],
        out_specs=pl.BlockSpec((tm, D_MODEL), row),
        out_shape=jax.ShapeDtypeStruct((t, D_MODEL), F32),
        compiler_params=_params(("parallel",)),
        name="final",
    )(x2, p2d, w_ple_gate.astype(BF16), w_ple_proj.astype(BF16),
      final_norm_g.reshape(1, -1))


def kernel(x, p, mix_norm_g, w_in, conv_w, conv_b, dt_bias, a_log, d_skip, ssd_norm_g, w_out_a, w_out_b, w_out, ffn_norm_g, w_rg, b_rg, w_re, b_re, w_gate, w_up, w_down, w_ple_proj, w_ple_gate, final_norm_g):
    bsz, seq, _ = x.shape
    assert w_in.shape[0] == 1, "single-layer block only"
    t = bsz * seq
    xr = x.reshape(t, D_MODEL)
    for i in range(1):
        qkv, z, xbc, dt_raw, gates = _in_proj(xr, mix_norm_g[i], w_in[i])
        attn = _moba(qkv, bsz, seq)
        ssd_out = _ssd(xbc, dt_raw, z, conv_w[i], conv_b[i], dt_bias[i], a_log[i],
                       d_skip[i], ssd_norm_g[i], bsz, seq)
        wr, br = _router_weights(w_rg[i], b_rg[i], w_re[i], b_re[i])
        x1e, route = _merge(xr, attn, ssd_out, gates, w_out_a[i], w_out_b[i], w_out[i],
                            ffn_norm_g[i], wr, br)
        cls = route[2].astype(jnp.int32)
        x2 = _moe(x1e, cls, ffn_norm_g[i], w_gate[i], w_up[i], w_down[i])
        xr = _final(x2, p[i].reshape(t, PLE_DIM), w_ple_gate[i], w_ple_proj[i],
                    final_norm_g)
    return xr.reshape(bsz, seq, D_MODEL)
```
